```python
import jax, jax.numpy as jnp
from jax import lax
import numpy as np

D_MODEL = 1024
BATCH = 32
SEQ = 2048
DEPTH = 2

CHUNK = 64
MIX_WIDTH = D_MODEL
SB_HEAD_DIM = 64
SB_WIDTH = MIX_WIDTH // 2
SB_HEADS = SB_WIDTH // SB_HEAD_DIM
SB_SCALE = SB_HEAD_DIM ** -0.5
Q_BLOCK = 128
SG_GROUP_DIM = 64
SG_WIDTH = MIX_WIDTH - SB_WIDTH
SG_GROUPS = SG_WIDTH // SG_GROUP_DIM
SG_BLOCK = 128
IN_PROJ_WIDTH = 3 * SB_WIDTH + 2 * SG_WIDTH
D_FF = ((8 * D_MODEL // 3 + 127) // 128) * 128
N_EXPERTS = 8
TOP_K = 2
D_EXPERT = 7 * D_MODEL // 2
MOE_BLOCK = 512
N_DENSE = (DEPTH + 1) // 2
N_MOE = DEPTH // 2
EPS = 1e-6

kernel_name = 'hybrid_stickbreak_gmlp_moe_adaln'


def rms_norm(x, g):
    xf = x.astype(jnp.float32)
    y = xf * lax.rsqrt(jnp.mean(xf * xf, axis=-1, keepdims=True) + EPS)
    return (y * g.astype(jnp.float32)).astype(x.dtype)


def layer_norm(x, g, b):
    xf = x.astype(jnp.float32)
    mu = jnp.mean(xf, axis=-1, keepdims=True)
    var = jnp.mean(jnp.square(xf - mu), axis=-1, keepdims=True)
    y = (xf - mu) * lax.rsqrt(var + EPS) * g.astype(jnp.float32) + b.astype(jnp.float32)
    return y.astype(x.dtype)


def stick_breaking_attention(q, k, v):
    S = q.shape[2]
    outs = []
    for i in range(S // Q_BLOCK):
        q0 = i * Q_BLOCK
        kv_len = q0 + Q_BLOCK
        qb = q[:, :, q0:kv_len]
        kb = k[:, :, :kv_len]
        vb = v[:, :, :kv_len]
        z = jnp.einsum('bhqd,bhkd->bhqk', qb, kb).astype(jnp.float32) * SB_SCALE
        q_pos = q0 + jnp.arange(Q_BLOCK)
        k_pos = jnp.arange(kv_len)
        strict = k_pos[None, :] < q_pos[:, None]
        log_keep = jnp.where(strict, jax.nn.log_sigmoid(-z), 0.0)
        log_after = lax.cumsum(log_keep, axis=3, reverse=True) - log_keep
        w = jnp.where(strict, jnp.exp(jax.nn.log_sigmoid(z) + log_after), 0.0)
        outs.append(jnp.einsum('bhqk,bhkd->bhqd', w.astype(vb.dtype), vb))
    return jnp.concatenate(outs, axis=2)


def spatial_gating(u, v, ln_g, ln_b, w_s, b_s):
    B, S, _ = v.shape
    v = layer_norm(v, ln_g, ln_b)
    nb = S // SG_BLOCK
    v = v.reshape(B, nb, SG_BLOCK, SG_GROUPS, SG_GROUP_DIM)
    pos = jnp.arange(SG_BLOCK)
    chunk_causal = (pos[None, :] // CHUNK) <= (pos[:, None] // CHUNK)
    w = jnp.where(chunk_causal[None], w_s, 0.0).astype(v.dtype)
    mixed = jnp.einsum('gij,bnjgc->bnigc', w, v) + b_s.T[None, None, :, :, None].astype(v.dtype)
    return u * mixed.reshape(B, S, SG_WIDTH)


def token_mixer(h, w_in, q_g, k_g, ln_g, ln_b, w_s, b_s, w_out):
    B, S, _ = h.shape
    proj = h @ w_in
    q, k, v, u_sg, v_sg = jnp.split(
        proj, [SB_WIDTH, 2 * SB_WIDTH, 3 * SB_WIDTH, 3 * SB_WIDTH + SG_WIDTH], axis=-1)
    to_heads = lambda t: t.reshape(B, S, SB_HEADS, SB_HEAD_DIM).transpose(0, 2, 1, 3)
    q = rms_norm(to_heads(q), q_g)
    k = rms_norm(to_heads(k), k_g)
    o_a = stick_breaking_attention(q, k, to_heads(v))
    o_a = o_a.transpose(0, 2, 1, 3).reshape(B, S, SB_WIDTH)
    o_b = spatial_gating(jax.nn.gelu(u_sg, approximate=False), jax.nn.gelu(v_sg, approximate=False),
                         ln_g, ln_b, w_s, b_s)
    return jnp.concatenate([o_a, o_b], axis=-1) @ w_out


def swiglu(h, w_gate, w_up, w_down):
    return (jax.nn.silu(h @ w_gate) * (h @ w_up)) @ w_down


def moe_swiglu(h, router_w, router_b, w_gate, w_up, w_down):
    B, S, D = h.shape
    T = B * S
    A = T * TOP_K
    t = h.reshape(T, D)
    logits = (t @ router_w).astype(jnp.float32) + router_b.astype(jnp.float32)
    top_logit, top_e = lax.top_k(logits, TOP_K)
    gate = jax.nn.softmax(top_logit, axis=-1).astype(h.dtype)
    flat_e = top_e.reshape(A)
    flat_tok = jnp.repeat(jnp.arange(T, dtype=jnp.int32), TOP_K)
    flat_gate = gate.reshape(A)
    order = jnp.argsort(flat_e)
    sorted_e = flat_e[order]
    counts = jnp.bincount(flat_e, length=N_EXPERTS)
    seg_start = jnp.cumsum(counts) - counts
    padded = (counts + MOE_BLOCK - 1) // MOE_BLOCK * MOE_BLOCK
    pad_end = jnp.cumsum(padded)
    pad_start = pad_end - padded
    dest = pad_start[sorted_e] + (jnp.arange(A) - seg_start[sorted_e])
    n_blocks = -(-A // MOE_BLOCK) + N_EXPERTS
    P = n_blocks * MOE_BLOCK
    slot_tok = jnp.zeros((P,), jnp.int32).at[dest].set(flat_tok[order])
    slot_gate = jnp.zeros((P,), h.dtype).at[dest].set(flat_gate[order])
    block_e = jnp.minimum(
        jnp.searchsorted(pad_end, jnp.arange(n_blocks) * MOE_BLOCK, side='right'), N_EXPERTS - 1)
    xb = t[slot_tok].reshape(n_blocks, MOE_BLOCK, D)

    def expert_block(args):
        xe, e = args
        return (jax.nn.silu(xe @ w_gate[e]) * (xe @ w_up[e])) @ w_down[e]

    yb = lax.map(expert_block, (xb, block_e)).reshape(P, D)
    y = jax.ops.segment_sum(yb * slot_gate[:, None], slot_tok, num_segments=T)
    return y.reshape(B, S, D)


def setup_inputs(seed: int = 0) -> dict:
    key = jax.random.key(seed)
    ks = jax.random.split(key, 24)
    nrm = lambda k, shape, s: jax.random.normal(k, shape, jnp.float32) * s
    D = D_MODEL
    return {
        'x': nrm(ks[0], (BATCH, SEQ, D), 1.0),
        'c': nrm(ks[1], (BATCH, D), 1.0),
        'ada_w': nrm(ks[2], (DEPTH, D, 6 * D), 0.5 * D ** -0.5),
        'ada_b': nrm(ks[3], (DEPTH, 6 * D), 0.02),
        'norm_mix_g': 1.0 + nrm(ks[4], (DEPTH, D), 0.02),
        'norm_ffn_g': 1.0 + nrm(ks[5], (DEPTH, D), 0.02),
        'w_in': nrm(ks[6], (DEPTH, D, IN_PROJ_WIDTH), D ** -0.5),
        'q_norm_g': 1.0 + nrm(ks[7], (DEPTH, SB_HEAD_DIM), 0.02),
        'k_norm_g': 1.0 + nrm(ks[8], (DEPTH, SB_HEAD_DIM), 0.02),
        'sg_ln_g': 1.0 + nrm(ks[9], (DEPTH, SG_WIDTH), 0.02),
        'sg_ln_b': nrm(ks[10], (DEPTH, SG_WIDTH), 0.02),
        'sg_w_spatial': nrm(ks[11], (DEPTH, SG_GROUPS, SG_BLOCK, SG_BLOCK), 0.5 * SG_BLOCK ** -0.5),
        'sg_b_spatial': 1.0 + nrm(ks[12], (DEPTH, SG_GROUPS, SG_BLOCK), 0.1),
        'w_out': nrm(ks[13], (DEPTH, MIX_WIDTH, D), MIX_WIDTH ** -0.5),
        'ffn_w_gate': nrm(ks[14], (N_DENSE, D, D_FF), D ** -0.5),
        'ffn_w_up': nrm(ks[15], (N_DENSE, D, D_FF), D ** -0.5),
        'ffn_w_down': nrm(ks[16], (N_DENSE, D_FF, D), D_FF ** -0.5),
        'router_w': nrm(ks[17], (N_MOE, D, N_EXPERTS), D ** -0.5),
        'router_b': nrm(ks[18], (N_MOE, N_EXPERTS), 0.01),
        'moe_w_gate': nrm(ks[19], (N_MOE, N_EXPERTS, D, D_EXPERT), D ** -0.5),
        'moe_w_up': nrm(ks[20], (N_MOE, N_EXPERTS, D, D_EXPERT), D ** -0.5),
        'moe_w_down': nrm(ks[21], (N_MOE, N_EXPERTS, D_EXPERT, D), D_EXPERT ** -0.5),
    }


def reference(x, c, ada_w, ada_b, norm_mix_g, norm_ffn_g, w_in, q_norm_g, k_norm_g,
              sg_ln_g, sg_ln_b, sg_w_spatial, sg_b_spatial, w_out,
              ffn_w_gate, ffn_w_up, ffn_w_down, router_w, router_b,
              moe_w_gate, moe_w_up, moe_w_down):
    c_act = jax.nn.silu(c)
    for l in range(DEPTH):
        ada = (c_act @ ada_w[l] + ada_b[l])[:, None, :]
        sh1, sc1, g1, sh2, sc2, g2 = jnp.split(ada, 6, axis=-1)
        h = rms_norm(x, norm_mix_g[l]) * (1.0 + sc1) + sh1
        x = x + g1 * token_mixer(h, w_in[l], q_norm_g[l], k_norm_g[l], sg_ln_g[l], sg_ln_b[l],
                                 sg_w_spatial[l], sg_b_spatial[l], w_out[l])
        h = rms_norm(x, norm_ffn_g[l]) * (1.0 + sc2) + sh2
        if l % 2 == 0:
            j = l // 2
            y = swiglu(h, ffn_w_gate[j], ffn_w_up[j], ffn_w_down[j])
        else:
            j = l // 2
            y = moe_swiglu(h, router_w[j], router_b[j], moe_w_gate[j], moe_w_up[j], moe_w_down[j])
        x = x + g2 * y
    return x
```

```python
import functools
import math

import jax
import jax.numpy as jnp
from jax import lax
from jax.experimental import pallas as pl
from jax.experimental.pallas import tpu as pltpu

F32 = jnp.float32
BF16 = jnp.bfloat16

HEAD_DIM = 64
PAIR = 2 * HEAD_DIM
SEQ_BLOCK = 128
CHUNK = 64
N_EXPERTS = 8
EPS = 1e-6

LANES = 128
MXU_DIM = 256
VMEM_LIMIT_BYTES = 56 * 1024 * 1024

TM_PROJ = 256
TM_FFN = 512
TM_MOE = 1024
TL_ROUTE = 512
TT_ROWS = 256


def _cparams(sem):
    return pltpu.CompilerParams(dimension_semantics=sem, vmem_limit_bytes=VMEM_LIMIT_BYTES)


def _split_bf16(x):
    hi = x.astype(BF16)
    lo = (x - hi.astype(F32)).astype(BF16)
    return hi, lo


def _largest_divisor(n, candidates):
    for c in candidates:
        if n % c == 0:
            return c
    raise ValueError(f"no tile in {candidates} divides {n}")


def _ada_kernel(c_ref, w_ref, b_ref, o_ref):
    c = c_ref[...]
    c_act = (c * jax.nn.sigmoid(c)).astype(BF16)
    o_ref[0] = jnp.dot(c_act, w_ref[0].astype(BF16), preferred_element_type=F32) + b_ref[0]


def _ada_call(c, ada_w, ada_b):
    depth, d, n = ada_w.shape
    b = c.shape[0]
    tn = _largest_divisor(n, (1536, 1024, 512, 256, 128))
    return pl.pallas_call(
        _ada_kernel,
        grid=(depth, n // tn),
        in_specs=[
            pl.BlockSpec((b, d), lambda l, j: (0, 0)),
            pl.BlockSpec((1, d, tn), lambda l, j: (l, 0, j)),
            pl.BlockSpec((1, 1, tn), lambda l, j: (l, 0, j)),
        ],
        out_specs=pl.BlockSpec((1, b, tn), lambda l, j: (l, 0, j)),
        out_shape=jax.ShapeDtypeStruct((depth, b, n), F32),
        compiler_params=_cparams(("arbitrary", "arbitrary")),
        name="ada_ln",
    )(c, ada_w, ada_b.reshape(depth, 1, n))


def _group_mean_sq(t, bd):
    wb = bd.shape[0]
    sq = t * t
    hi, lo = _split_bf16(sq)
    cols = []
    for c0 in range(0, t.shape[1], wb):
        s = jnp.dot(hi[:, c0:c0 + wb], bd, preferred_element_type=F32)
        s = s + jnp.dot(lo[:, c0:c0 + wb], bd, preferred_element_type=F32)
        cols.append(s)
    ss = cols[0] if len(cols) == 1 else jnp.concatenate(cols, axis=1)
    return ss * (1.0 / HEAD_DIM)


def _gelu(t):
    return 0.5 * t * (1.0 + lax.erf(t * math.sqrt(0.5)))


def _inproj_kernel(x_ref, mod_ref, ng_ref, w_ref, qkg_ref, bd_ref, ln_ref, ws_ref, bias_ref,
                   q_ref, k_ref, v_ref, ob_ref, *, sbw, sgw):
    xf = x_ref[0]
    ms = jnp.mean(xf * xf, axis=-1, keepdims=True)
    y = xf * lax.rsqrt(ms + EPS) * ng_ref[...]
    h = (y * (1.0 + mod_ref[0, 1:2, :]) + mod_ref[0, 0:1, :]).astype(BF16)

    bd = bd_ref[...]
    q = jnp.dot(h, w_ref[:, 0:sbw], preferred_element_type=F32)
    qn = q * lax.rsqrt(_group_mean_sq(q, bd) + EPS) * qkg_ref[0:1, :]
    q_ref[0] = qn.astype(BF16)
    k = jnp.dot(h, w_ref[:, sbw:2 * sbw], preferred_element_type=F32)
    kn = k * lax.rsqrt(_group_mean_sq(k, bd) + EPS) * qkg_ref[1:2, :]
    k_ref[0] = kn.astype(BF16)
    v_ref[0] = jnp.dot(h, w_ref[:, 2 * sbw:3 * sbw], preferred_element_type=F32).astype(BF16)

    u = _gelu(jnp.dot(h, w_ref[:, 3 * sbw:3 * sbw + sgw], preferred_element_type=F32))
    g = _gelu(jnp.dot(h, w_ref[:, 3 * sbw + sgw:3 * sbw + 2 * sgw], preferred_element_type=F32))
    mu = jnp.mean(g, axis=-1, keepdims=True)
    gc = g - mu
    var = jnp.mean(gc * gc, axis=-1, keepdims=True)
    gn = (gc * lax.rsqrt(var + EPS) * ln_ref[0:1, :] + ln_ref[1:2, :]).astype(BF16)

    tm = xf.shape[0]
    first = lax.broadcasted_iota(jnp.int32, (SEQ_BLOCK, PAIR), 1) < HEAD_DIM
    zero = jnp.zeros((SEQ_BLOCK, PAIR), BF16)
    for r in range(tm // SEQ_BLOCK):
        rows = slice(r * SEQ_BLOCK, (r + 1) * SEQ_BLOCK)
        for p in range(sgw // PAIR):
            cols = slice(p * PAIR, (p + 1) * PAIR)
            vb = gn[rows, cols]
            rhs = jnp.concatenate([jnp.where(first, vb, zero), jnp.where(first, zero, vb)], axis=0)
            mixed = jnp.dot(ws_ref[p], rhs, preferred_element_type=F32)
            ob_ref[0, rows, cols] = (u[rows, cols] * (mixed + bias_ref[:, cols])).astype(BF16)


def _inproj_call(x, mod, norm_g, w_in, qk_g, bd, ln, ws, bias, *, sbw, sgw):
    b, s, d = x.shape
    tm = min(TM_PROJ, s)
    n_in = w_in.shape[1]
    out = jax.ShapeDtypeStruct((b, s, sbw), BF16)
    row_spec = lambda w: pl.BlockSpec((1, tm, w), lambda i, j: (i, j, 0))
    const = lambda shape: pl.BlockSpec(shape, lambda i, j: (0,) * len(shape))
    return pl.pallas_call(
        functools.partial(_inproj_kernel, sbw=sbw, sgw=sgw),
        grid=(b, s // tm),
        in_specs=[
            row_spec(d),
            pl.BlockSpec((1, 2, d), lambda i, j: (i, 0, 0)),
            const((1, d)),
            const((d, n_in)),
            const((2, sbw)),
            const(bd.shape),
            const((2, sgw)),
            const(ws.shape),
            const((SEQ_BLOCK, sgw)),
        ],
        out_specs=[row_spec(sbw), row_spec(sbw), row_spec(sbw), row_spec(sgw)],
        out_shape=[out, out, out, jax.ShapeDtypeStruct((b, s, sgw), BF16)],
        compiler_params=_cparams(("parallel", "arbitrary")),
        name="mixer_in",
    )(x, mod, norm_g, w_in, qk_g, bd, ln, ws, bias)


def _attn_kernel(q_ref, k_ref, v_ref, o_ref, acc_ref, carry_ref):
    i = pl.program_id(2)
    tq = SEQ_BLOCK
    q = q_ref[0]
    first = lax.broadcasted_iota(jnp.int32, (tq, PAIR), 1) < HEAD_DIM
    zero = jnp.zeros_like(q)
    q2 = jnp.concatenate([jnp.where(first, q, zero), jnp.where(first, zero, q)], axis=0)

    jj = lax.broadcasted_iota(jnp.int32, (tq, 2 * tq), 0)
    ss = lax.broadcasted_iota(jnp.int32, (tq, 2 * tq), 1)
    csum = jnp.where((jj > ss) | (ss >= tq), 1.0, 0.0).astype(BF16)

    qpos = lax.broadcasted_iota(jnp.int32, (2 * tq, tq), 0) & (tq - 1)
    kpos = lax.broadcasted_iota(jnp.int32, (2 * tq, tq), 1)
    strict = kpos < qpos

    def tile(j, diagonal):
        start = pl.multiple_of(j * tq, tq)
        kb = k_ref[0, pl.ds(start, tq), :]
        vb = v_ref[0, pl.ds(start, tq), :]
        z = lax.dot_general(q2, kb, (((1,), (1,)), ((), ())), preferred_element_type=F32)
        sp = jnp.maximum(z, 0.0) + jnp.log(1.0 + jnp.exp(-jnp.abs(z)))
        lk = -sp
        if diagonal:
            lk = jnp.where(strict, lk, 0.0)
        hi, lo = _split_bf16(lk)
        cs = jnp.dot(hi, csum, preferred_element_type=F32) + jnp.dot(lo, csum, preferred_element_type=F32)
        if diagonal:
            w = jnp.where(strict, jnp.exp(z - sp + cs[:, :tq]), 0.0)
            carry_ref[...] = cs[:, tq:]
            acc_ref[...] = jnp.dot(w.astype(BF16), vb, preferred_element_type=F32)
        else:
            w = jnp.exp(z - sp + cs[:, :tq] + carry_ref[...])
            carry_ref[...] += cs[:, tq:]
            acc_ref[...] += jnp.dot(w.astype(BF16), vb, preferred_element_type=F32)

    tile(i, True)

    def body(t, c):
        tile(i - 1 - t, False)
        return c

    lax.fori_loop(0, i, body, 0)
    o_ref[0] = jnp.where(first, acc_ref[0:tq, :], acc_ref[tq:2 * tq, :]).astype(BF16)


def _attn_call(q, k, v):
    b, s, w = q.shape
    tq = SEQ_BLOCK
    return pl.pallas_call(
        _attn_kernel,
        grid=(b, w // PAIR, s // tq),
        in_specs=[
            pl.BlockSpec((1, tq, PAIR), lambda bi, p, i: (bi, i, p)),
            pl.BlockSpec((1, s, PAIR), lambda bi, p, i: (bi, 0, p)),
            pl.BlockSpec((1, s, PAIR), lambda bi, p, i: (bi, 0, p)),
        ],
        out_specs=pl.BlockSpec((1, tq, PAIR), lambda bi, p, i: (bi, i, p)),
        out_shape=jax.ShapeDtypeStruct((b, s, w), BF16),
        scratch_shapes=[pltpu.VMEM((2 * tq, PAIR), F32), pltpu.VMEM((2 * tq, tq), F32)],
        compiler_params=_cparams(("parallel", "parallel", "arbitrary")),
        name="sb_attn",
    )(q, k, v)


def _outproj_kernel(*refs, sbw, with_router):
    if with_router:
        oa_ref, ob_ref, x_ref, mod_ref, ng_ref, w_ref, rw_ref, rb_ref, x1_ref, h_ref, lg_ref = refs
    else:
        oa_ref, ob_ref, x_ref, mod_ref, ng_ref, w_ref, x1_ref, h_ref = refs
    y = jnp.dot(oa_ref[0], w_ref[0:sbw, :], preferred_element_type=F32)
    y = y + jnp.dot(ob_ref[0], w_ref[sbw:, :], preferred_element_type=F32)
    x1 = x_ref[0] + mod_ref[0, 0:1, :] * y
    x1_ref[0] = x1
    ms = jnp.mean(x1 * x1, axis=-1, keepdims=True)
    hn = x1 * lax.rsqrt(ms + EPS) * ng_ref[...]
    h = (hn * (1.0 + mod_ref[0, 2:3, :]) + mod_ref[0, 1:2, :]).astype(BF16)
    h_ref[0] = h
    if with_router:
        lg = lax.dot_general(rw_ref[...], h, (((1,), (1,)), ((), ())), preferred_element_type=F32)
        lg_ref[...] = lg[0:N_EXPERTS, :] + rb_ref[...]


def _outproj_call(oa, ob, x, mod, norm_g, w_out, router=None):
    b, s, d = x.shape
    sbw = oa.shape[-1]
    tm = min(TM_PROJ, s)
    row_spec = lambda w: pl.BlockSpec((1, tm, w), lambda i, j: (i, j, 0))
    const = lambda shape: pl.BlockSpec(shape, lambda i, j: (0,) * len(shape))
    in_specs = [row_spec(sbw), row_spec(ob.shape[-1]), row_spec(d),
                pl.BlockSpec((1, 3, d), lambda i, j: (i, 0, 0)), const((1, d)), const(w_out.shape)]
    out_specs = [row_spec(d), row_spec(d)]
    out_shape = [jax.ShapeDtypeStruct((b, s, d), F32), jax.ShapeDtypeStruct((b, s, d), BF16)]
    args = [oa, ob, x, mod, norm_g, w_out]
    if router is not None:
        rw_t, rb = router
        in_specs += [const(rw_t.shape), const(rb.shape)]
        nj = s // tm
        out_specs.append(pl.BlockSpec((N_EXPERTS, tm), lambda i, j: (0, i * nj + j)))
        out_shape.append(jax.ShapeDtypeStruct((N_EXPERTS, b * s), F32))
        args += [rw_t, rb]
    return pl.pallas_call(
        functools.partial(_outproj_kernel, sbw=sbw, with_router=router is not None),
        grid=(b, s // tm),
        in_specs=in_specs,
        out_specs=out_specs,
        out_shape=out_shape,
        compiler_params=_cparams(("parallel", "arbitrary")),
        name="mixer_out",
    )(*args)


def _ffn_kernel(h_ref, x_ref, g2_ref, wg_ref, wu_ref, wd_ref, o_ref, acc_ref):
    h = h_ref[0]
    n_chunks = wg_ref.shape[0]

    def body(c, carry):
        g = jnp.dot(h, wg_ref[c], preferred_element_type=F32)
        u = jnp.dot(h, wu_ref[c], preferred_element_type=F32)
        a = (g * jax.nn.sigmoid(g) * u).astype(BF16)
        acc_ref[...] += jnp.dot(a, wd_ref[c], preferred_element_type=F32)
        return carry

    acc_ref[...] = jnp.zeros_like(acc_ref)
    lax.fori_loop(0, n_chunks, body, 0)
    o_ref[0] = x_ref[0] + g2_ref[0] * acc_ref[...]


def _ffn_call(h, x, g2, wg, wu, wd):
    b, s, d = x.shape
    tm = min(TM_FFN, s)
    row_spec = pl.BlockSpec((1, tm, d), lambda i, j: (i, j, 0))
    resident = lambda a: pl.BlockSpec(a.shape, lambda i, j: (0, 0, 0), pipeline_mode=pl.Buffered(1))
    return pl.pallas_call(
        _ffn_kernel,
        grid=(b, s // tm),
        in_specs=[row_spec, row_spec, pl.BlockSpec((1, 1, d), lambda i, j: (i, 0, 0)),
                  resident(wg), resident(wu), resident(wd)],
        out_specs=row_spec,
        out_shape=jax.ShapeDtypeStruct((b, s, d), F32),
        scratch_shapes=[pltpu.VMEM((tm, d), F32)],
        compiler_params=_cparams(("parallel", "arbitrary")),
        name="dense_ffn",
    )(h, x, g2, wg, wu, wd)


def _route_kernel(lg_ref, oi_ref, of_ref, cnt_ref, carry_ref):
    step = pl.program_id(0)
    tl = lg_ref.shape[1]
    ne = N_EXPERTS

    @pl.when(step == 0)
    def _():
        carry_ref[...] = jnp.zeros_like(carry_ref)

    lg = lg_ref[...]
    eidx = lax.broadcasted_iota(jnp.int32, (ne, tl), 0).astype(F32)
    m1 = jnp.max(lg, axis=0, keepdims=True)
    i1 = jnp.min(jnp.where(lg == m1, eidx, float(ne)), axis=0, keepdims=True)
    lg2 = jnp.where(eidx == i1, -jnp.inf, lg)
    m2 = jnp.max(lg2, axis=0, keepdims=True)
    i2 = jnp.min(jnp.where(lg2 == m2, eidx, float(ne)), axis=0, keepdims=True)
    t = jnp.exp(m2 - m1)
    gate1 = 1.0 / (1.0 + t)
    gate2 = t / (1.0 + t)

    sel1 = eidx == i1
    sel2 = eidx == i2
    onehot = jnp.where(sel1 | sel2, 1.0, 0.0)
    a = lax.broadcasted_iota(jnp.int32, (tl, tl), 0)
    bcol = lax.broadcasted_iota(jnp.int32, (tl, tl), 1)
    upper = jnp.where(a < bcol, 1.0, 0.0).astype(BF16)
    lhs = jnp.concatenate([onehot, jnp.zeros_like(onehot)], axis=0).astype(BF16)
    excl = jnp.dot(lhs, upper, preferred_element_type=F32)[0:ne, :] + carry_ref[:, 0:1]
    r1 = jnp.sum(jnp.where(sel1, excl, 0.0), axis=0, keepdims=True)
    r2 = jnp.sum(jnp.where(sel2, excl, 0.0), axis=0, keepdims=True)
    carry_ref[...] += jnp.sum(onehot, axis=1, keepdims=True)

    zi = jnp.zeros((ne - 4, tl), jnp.int32)
    oi_ref[...] = jnp.concatenate(
        [i1.astype(jnp.int32), i2.astype(jnp.int32), r1.astype(jnp.int32), r2.astype(jnp.int32), zi], axis=0)
    of_ref[...] = jnp.concatenate([gate1, gate2, jnp.zeros((ne - 2, tl), F32)], axis=0)
    cnt_ref[...] = carry_ref[...]


def _route_call(logits_t):
    ne, t = logits_t.shape
    tl = min(TL_ROUTE, t)
    blk = pl.BlockSpec((ne, tl), lambda i: (0, i))
    return pl.pallas_call(
        _route_kernel,
        grid=(t // tl,),
        in_specs=[blk],
        out_specs=[blk, blk, pl.BlockSpec((ne, LANES), lambda i: (0, 0))],
        out_shape=[jax.ShapeDtypeStruct((ne, t), jnp.int32), jax.ShapeDtypeStruct((ne, t), F32),
                   jax.ShapeDtypeStruct((ne, LANES), F32)],
        scratch_shapes=[pltpu.VMEM((ne, LANES), F32)],
        compiler_params=_cparams(("arbitrary",)),
        name="moe_route",
    )(logits_t)


def _dispatch_kernel(tail_ref, dest_ref, h_ref, xb_ref, zero_ref, sem, zsem):
    step = pl.program_id(0)
    tt = dest_ref.shape[2]
    tmb = zero_ref.shape[0]

    def zero_block(blk):
        row = pl.multiple_of(blk * tmb, tmb)
        return pltpu.make_async_copy(zero_ref, xb_ref.at[pl.ds(row, tmb)], zsem)

    @pl.when(step == 0)
    def _():
        zero_ref[...] = jnp.zeros_like(zero_ref)
        for e in range(N_EXPERTS):
            zero_block(tail_ref[e]).start()
        for e in range(N_EXPERTS):
            zero_block(tail_ref[e]).wait()

        def zero_unused(blk, c):
            cp = zero_block(blk)
            cp.start()
            cp.wait()
            return c

        lax.fori_loop(tail_ref[N_EXPERTS], xb_ref.shape[0] // tmb, zero_unused, 0)

    base = step * tt

    def row_copy(t, k):
        return pltpu.make_async_copy(h_ref.at[pl.ds(base + t, 1)],
                                     xb_ref.at[pl.ds(dest_ref[0, k, t], 1)], sem)

    def start(t, c):
        row_copy(t, 0).start()
        row_copy(t, 1).start()
        return c

    def wait(t, c):
        row_copy(t, 0).wait()
        row_copy(t, 1).wait()
        return c

    lax.fori_loop(0, tt, start, 0, unroll=8)
    lax.fori_loop(0, tt, wait, 0, unroll=8)


def _dispatch_call(tail_blocks, dest, h_words, n_slots):
    t, wd = h_words.shape
    nsteps, _, tt = dest.shape
    tmb = TM_MOE
    return pl.pallas_call(
        _dispatch_kernel,
        grid_spec=pltpu.PrefetchScalarGridSpec(
            num_scalar_prefetch=1,
            grid=(nsteps,),
            in_specs=[pl.BlockSpec((1, 2, tt), lambda i, tail: (i, 0, 0), memory_space=pltpu.SMEM),
                      pl.BlockSpec(memory_space=pl.ANY)],
            out_specs=pl.BlockSpec(memory_space=pl.ANY),
            scratch_shapes=[pltpu.VMEM((tmb, wd), jnp.uint32), pltpu.SemaphoreType.DMA,
                            pltpu.SemaphoreType.DMA],
        ),
        out_shape=jax.ShapeDtypeStruct((n_slots, wd), jnp.uint32),
        compiler_params=pltpu.CompilerParams(dimension_semantics=("arbitrary",),
                                             vmem_limit_bytes=VMEM_LIMIT_BYTES, has_side_effects=True),
        name="moe_dispatch",
    )(tail_blocks, dest, h_words)


def _moe_ffn_kernel(be_ref, nb_ref, x_ref, wg_ref, wu_ref, wd_ref, o_ref, acc_ref):
    i = pl.program_id(0)
    c = pl.program_id(1)

    @pl.when(i < nb_ref[0])
    def _():
        @pl.when(c == 0)
        def _():
            acc_ref[...] = jnp.zeros_like(acc_ref)

        xb = x_ref[...]
        g = jnp.dot(xb, wg_ref[0], preferred_element_type=F32)
        u = jnp.dot(xb, wu_ref[0], preferred_element_type=F32)
        a = (g * jax.nn.sigmoid(g) * u).astype(BF16)
        acc_ref[...] += jnp.dot(a, wd_ref[0], preferred_element_type=F32)

        @pl.when(c == pl.num_programs(1) - 1)
        def _():
            o_ref[...] = acc_ref[...]

    @pl.when((i >= nb_ref[0]) & (c == pl.num_programs(1) - 1))
    def _():
        o_ref[...] = jnp.zeros_like(o_ref)


def _moe_ffn_call(block_e, n_used, xb, wg, wu, wd):
    n_slots, d = xb.shape
    tmb = TM_MOE
    nb = n_slots // tmb
    f = wg.shape[2]
    tf = _largest_divisor(f, (512, 256, 128))
    nc = f // tf
    blk = lambda i, nbu: jnp.minimum(i, nbu[0] - 1)
    chunk = lambda i, c, nbu: jnp.where(i < nbu[0], c, nc - 1)
    return pl.pallas_call(
        _moe_ffn_kernel,
        grid_spec=pltpu.PrefetchScalarGridSpec(
            num_scalar_prefetch=2,
            grid=(nb, nc),
            in_specs=[
                pl.BlockSpec((tmb, d), lambda i, c, be, nbu: (blk(i, nbu), 0)),
                pl.BlockSpec((1, d, tf), lambda i, c, be, nbu: (be[blk(i, nbu)], 0, chunk(i, c, nbu))),
                pl.BlockSpec((1, d, tf), lambda i, c, be, nbu: (be[blk(i, nbu)], 0, chunk(i, c, nbu))),
                pl.BlockSpec((1, tf, d), lambda i, c, be, nbu: (be[blk(i, nbu)], chunk(i, c, nbu), 0)),
            ],
            out_specs=pl.BlockSpec((tmb, d), lambda i, c, be, nbu: (i, 0)),
            scratch_shapes=[pltpu.VMEM((tmb, d), F32)],
        ),
        out_shape=jax.ShapeDtypeStruct((n_slots, d), F32),
        compiler_params=_cparams(("arbitrary", "arbitrary")),
        name="moe_ffn",
    )(block_e, n_used, xb, wg, wu, wd)


def _combine_kernel(dest_ref, yb_ref, x_ref, g2_ref, gate_ref, o_ref, rows_ref, sem):
    tt = x_ref.shape[1]

    def row_copy(t, k):
        return pltpu.make_async_copy(yb_ref.at[pl.ds(dest_ref[0, k, t], 1)],
                                     rows_ref.at[k, pl.ds(t, 1)], sem)

    def start(t, c):
        row_copy(t, 0).start()
        row_copy(t, 1).start()
        return c

    def wait(t, c):
        row_copy(t, 0).wait()
        row_copy(t, 1).wait()
        return c

    lax.fori_loop(0, tt, start, 0, unroll=8)
    lax.fori_loop(0, tt, wait, 0, unroll=8)
    y = rows_ref[0] * gate_ref[0, :, 0:1] + rows_ref[1] * gate_ref[0, :, 1:2]
    o_ref[0] = x_ref[0] + g2_ref[0] * y


def _combine_call(dest, yb, x, g2, gates):
    b, s, d = x.shape
    tt = dest.shape[2]
    nj = s // tt
    row_spec = pl.BlockSpec((1, tt, d), lambda i, j: (i, j, 0))
    return pl.pallas_call(
        _combine_kernel,
        grid=(b, nj),
        in_specs=[
            pl.BlockSpec((1, 2, tt), lambda i, j: (i * nj + j, 0, 0), memory_space=pltpu.SMEM),
            pl.BlockSpec(memory_space=pl.ANY),
            row_spec,
            pl.BlockSpec((1, 1, d), lambda i, j: (i, 0, 0)),
            pl.BlockSpec((1, tt, 2), lambda i, j: (i * nj + j, 0, 0)),
        ],
        out_specs=row_spec,
        out_shape=jax.ShapeDtypeStruct((b, s, d), F32),
        scratch_shapes=[pltpu.VMEM((2, tt, d), F32), pltpu.SemaphoreType.DMA],
        compiler_params=_cparams(("arbitrary", "arbitrary")),
        name="moe_combine",
    )(dest, yb, x, g2, gates)


def _chunk_cols(w, tf):
    d, f = w.shape
    return w.reshape(d, f // tf, tf).transpose(1, 0, 2).astype(BF16)


def _moe_layer(h, x1, g2, logits_t, wg, wu, wd):
    b, s, d = x1.shape
    t = b * s
    tmb = TM_MOE
    tt = min(TT_ROWS, s)
    ne = N_EXPERTS
    oi, of, cnt = _route_call(logits_t)
    e1, e2, r1, r2 = oi[0], oi[1], oi[2], oi[3]
    counts = cnt[:, 0].astype(jnp.int32)
    padded = (counts + tmb - 1) // tmb * tmb
    pad_end = jnp.cumsum(padded)
    pad_start = pad_end - padded
    n_used = (pad_end[-1] // tmb).astype(jnp.int32)
    nb = -(-(2 * t) // tmb) + ne
    block_row0 = jnp.arange(nb, dtype=jnp.int32) * tmb
    block_e = jnp.minimum(jnp.sum(pad_end[None, :] <= block_row0[:, None], axis=1), ne - 1).astype(jnp.int32)
    tail_blocks = jnp.where(padded > 0, pad_end // tmb - 1, nb - 1)
    tail_blocks = jnp.concatenate([tail_blocks, n_used[None]]).astype(jnp.int32)
    start_of = lambda e: jnp.sum(jnp.where(e[None, :] == jnp.arange(ne)[:, None], pad_start[:, None], 0), axis=0)
    d1 = start_of(e1) + r1
    d2 = start_of(e2) + r2
    dest = jnp.stack([d1.reshape(t // tt, tt), d2.reshape(t // tt, tt)], axis=1).astype(jnp.int32)

    h_words = lax.bitcast_convert_type(h.reshape(t, d // 2, 2), jnp.uint32)
    xb_words = _dispatch_call(tail_blocks, dest, h_words, nb * tmb)
    xb = lax.bitcast_convert_type(xb_words, BF16).reshape(nb * tmb, d)
    yb = _moe_ffn_call(block_e, n_used.reshape(1), xb, wg, wu, wd)
    gates = jnp.stack([of[0], of[1]], axis=-1).reshape(t // tt, tt, 2)
    return _combine_call(dest, yb, x1, g2, gates)


def kernel(x, c, ada_w, ada_b, norm_mix_g, norm_ffn_g, w_in, q_norm_g, k_norm_g, sg_ln_g, sg_ln_b,
           sg_w_spatial, sg_b_spatial, w_out, ffn_w_gate, ffn_w_up, ffn_w_down, router_w, router_b,
           moe_w_gate, moe_w_up, moe_w_down):
    b, s, d = x.shape
    depth = ada_w.shape[0]
    sgw = sg_ln_g.shape[-1]
    sbw = w_out.shape[1] - sgw
    assert sbw % PAIR == 0 and sgw % PAIR == 0 and s % SEQ_BLOCK == 0
    assert w_in.shape[-1] == 3 * sbw + 2 * sgw

    ada = _ada_call(c, ada_w, ada_b)
    ada = ada.reshape(depth, b, 6, d)

    wb = MXU_DIM if sbw % MXU_DIM == 0 else PAIR
    gi = jnp.arange(wb) // HEAD_DIM
    bd = (gi[:, None] == gi[None, :]).astype(BF16)
    pos = jnp.arange(SEQ_BLOCK)
    chunk_causal = (pos[None, :] // CHUNK) <= (pos[:, None] // CHUNK)
    scale = HEAD_DIM ** -0.5

    for l in range(depth):
        sh1, sc1, g1, sh2, sc2, g2 = (ada[l, :, j] for j in range(6))
        qk_g = jnp.stack([jnp.tile(q_norm_g[l], sbw // HEAD_DIM) * scale,
                          jnp.tile(k_norm_g[l], sbw // HEAD_DIM)])
        ln = jnp.stack([sg_ln_g[l], sg_ln_b[l]])
        ws = jnp.where(chunk_causal[None], sg_w_spatial[l], 0.0)
        ws = ws.reshape(sgw // PAIR, 2, SEQ_BLOCK, SEQ_BLOCK).transpose(0, 2, 1, 3)
        ws = ws.reshape(sgw // PAIR, SEQ_BLOCK, 2 * SEQ_BLOCK).astype(BF16)
        bias = jnp.repeat(sg_b_spatial[l].T, HEAD_DIM, axis=1)

        q, k, v, ob = _inproj_call(x, jnp.stack([sh1, sc1], axis=1), norm_mix_g[l][None],
                                   w_in[l].astype(BF16), qk_g, bd, ln, ws, bias, sbw=sbw, sgw=sgw)
        oa = _attn_call(q, k, v)
        mod2 = jnp.stack([g1, sh2, sc2], axis=1)
        j = l // 2
        if l % 2 == 0:
            x1, h = _outproj_call(oa, ob, x, mod2, norm_ffn_g[l][None], w_out[l].astype(BF16))
            tf = _largest_divisor(ffn_w_gate.shape[-1], (512, 256, 128))
            wg = _chunk_cols(ffn_w_gate[j], tf)
            wu = _chunk_cols(ffn_w_up[j], tf)
            wd = ffn_w_down[j].reshape(-1, tf, d).astype(BF16)
            x = _ffn_call(h, x1, g2[:, None], wg, wu, wd)
        else:
            rw_t = jnp.zeros((2 * N_EXPERTS, d), BF16).at[:N_EXPERTS].set(router_w[j].T.astype(BF16))
            rb = router_b[j].reshape(N_EXPERTS, 1)
            x1, h, logits_t = _outproj_call(oa, ob, x, mod2, norm_ffn_g[l][None],
                                            w_out[l].astype(BF16), router=(rw_t, rb))
            x = _moe_layer(h, x1, g2[:, None], logits_t, moe_w_gate[j].astype(BF16),
                           moe_w_up[j].astype(BF16), moe_w_down[j].astype(BF16))
    return x
```

```python
import functools
import math

import jax
import jax.numpy as jnp
from jax import lax
from jax.experimental import pallas as pl
from jax.experimental.pallas import tpu as pltpu

F32 = jnp.float32
BF16 = jnp.bfloat16

HEAD_DIM = 64
PAIR = 2 * HEAD_DIM
SEQ_BLOCK = 128
CHUNK = 64
N_EXPERTS = 8
EPS = 1e-6
LOG2_E = 1.4426950408889634

LANES = 128
SUBLANES = 8
MXU_DIM = 256
VMEM_LIMIT_BYTES = 56 * 1024 * 1024

ATT_TILE = MXU_DIM
ATT_PAIRS = 4
TM_PROJ = 512
TM_FFN = 512
TM_MOE = 1024
TT_ROWS = 512


def _cparams(sem):
    return pltpu.CompilerParams(dimension_semantics=sem, vmem_limit_bytes=VMEM_LIMIT_BYTES)


def _split_bf16(x):
    hi = x.astype(BF16)
    lo = (x - hi.astype(F32)).astype(BF16)
    return hi, lo


def _largest_divisor(n, candidates):
    for c in candidates:
        if n % c == 0:
            return c
    raise ValueError(f"no tile in {candidates} divides {n}")


def _ada_kernel(c_ref, w_ref, b_ref, o_ref):
    c = c_ref[...]
    c_act = (c * jax.nn.sigmoid(c)).astype(BF16)
    o_ref[0] = jnp.dot(c_act, w_ref[0].astype(BF16), preferred_element_type=F32) + b_ref[0]


def _ada_call(c, ada_w, ada_b):
    depth, d, n = ada_w.shape
    b = c.shape[0]
    tn = _largest_divisor(n, (1536, 1024, 512, 256, 128))
    return pl.pallas_call(
        _ada_kernel,
        grid=(depth, n // tn),
        in_specs=[
            pl.BlockSpec((b, d), lambda l, j: (0, 0)),
            pl.BlockSpec((1, d, tn), lambda l, j: (l, 0, j)),
            pl.BlockSpec((1, 1, tn), lambda l, j: (l, 0, j)),
        ],
        out_specs=pl.BlockSpec((1, b, tn), lambda l, j: (l, 0, j)),
        out_shape=jax.ShapeDtypeStruct((depth, b, n), F32),
        compiler_params=_cparams(("arbitrary", "arbitrary")),
        name="ada_ln",
    )(c, ada_w, ada_b.reshape(depth, 1, n))


def _group_mean_sq(t, bd):
    wb = bd.shape[0]
    sq = t * t
    hi, lo = _split_bf16(sq)
    cols = []
    for c0 in range(0, t.shape[1], wb):
        s = jnp.dot(hi[:, c0:c0 + wb], bd, preferred_element_type=F32)
        s = s + jnp.dot(lo[:, c0:c0 + wb], bd, preferred_element_type=F32)
        cols.append(s)
    ss = cols[0] if len(cols) == 1 else jnp.concatenate(cols, axis=1)
    return ss * (1.0 / HEAD_DIM)


def _gelu(t):
    return 0.5 * t * (1.0 + lax.erf(t * math.sqrt(0.5)))


def _inproj_kernel(x_ref, mod_ref, ng_ref, w_ref, qkg_ref, bd_ref, ln_ref, ws_ref, bias_ref,
                   q_ref, k_ref, v_ref, ob_ref, *, sbw, sgw):
    xf = x_ref[0]
    ms = jnp.mean(xf * xf, axis=-1, keepdims=True)
    y = xf * lax.rsqrt(ms + EPS) * ng_ref[...]
    h = (y * (1.0 + mod_ref[0, 1:2, :]) + mod_ref[0, 0:1, :]).astype(BF16)

    bd = bd_ref[...]
    q = jnp.dot(h, w_ref[:, 0:sbw], preferred_element_type=F32)
    qn = q * lax.rsqrt(_group_mean_sq(q, bd) + EPS) * qkg_ref[0:1, :]
    q_ref[0] = qn.astype(BF16)
    k = jnp.dot(h, w_ref[:, sbw:2 * sbw], preferred_element_type=F32)
    kn = k * lax.rsqrt(_group_mean_sq(k, bd) + EPS) * qkg_ref[1:2, :]
    k_ref[0] = kn.astype(BF16)
    v_ref[0] = jnp.dot(h, w_ref[:, 2 * sbw:3 * sbw], preferred_element_type=F32).astype(BF16)

    u = _gelu(jnp.dot(h, w_ref[:, 3 * sbw:3 * sbw + sgw], preferred_element_type=F32))
    g = _gelu(jnp.dot(h, w_ref[:, 3 * sbw + sgw:3 * sbw + 2 * sgw], preferred_element_type=F32))
    mu = jnp.mean(g, axis=-1, keepdims=True)
    gc = g - mu
    var = jnp.mean(gc * gc, axis=-1, keepdims=True)
    gn = (gc * lax.rsqrt(var + EPS) * ln_ref[0:1, :] + ln_ref[1:2, :]).astype(BF16)

    tm = xf.shape[0]
    first = lax.broadcasted_iota(jnp.int32, (SEQ_BLOCK, PAIR), 1) < HEAD_DIM
    zero = jnp.zeros((SEQ_BLOCK, PAIR), BF16)
    for r in range(tm // SEQ_BLOCK):
        rows = slice(r * SEQ_BLOCK, (r + 1) * SEQ_BLOCK)
        for p in range(sgw // PAIR):
            cols = slice(p * PAIR, (p + 1) * PAIR)
            vb = gn[rows, cols]
            rhs = jnp.concatenate([jnp.where(first, vb, zero), jnp.where(first, zero, vb)], axis=0)
            mixed = jnp.dot(ws_ref[p], rhs, preferred_element_type=F32)
            ob_ref[0, rows, cols] = (u[rows, cols] * (mixed + bias_ref[:, cols])).astype(BF16)


def _inproj_call(x, mod, norm_g, w_in, qk_g, bd, ln, ws, bias, *, sbw, sgw):
    b, s, d = x.shape
    tm = min(TM_PROJ, s)
    n_in = w_in.shape[1]
    out = jax.ShapeDtypeStruct((b, s, sbw), BF16)
    row_spec = lambda w: pl.BlockSpec((1, tm, w), lambda i, j: (i, j, 0))
    const = lambda shape: pl.BlockSpec(shape, lambda i, j: (0,) * len(shape))
    return pl.pallas_call(
        functools.partial(_inproj_kernel, sbw=sbw, sgw=sgw),
        grid=(b, s // tm),
        in_specs=[
            row_spec(d),
            pl.BlockSpec((1, 2, d), lambda i, j: (i, 0, 0)),
            const((1, d)),
            const((d, n_in)),
            const((2, sbw)),
            const(bd.shape),
            const((2, sgw)),
            const(ws.shape),
            const((SEQ_BLOCK, sgw)),
        ],
        out_specs=[row_spec(sbw), row_spec(sbw), row_spec(sbw), row_spec(sgw)],
        out_shape=[out, out, out, jax.ShapeDtypeStruct((b, s, sgw), BF16)],
        compiler_params=_cparams(("parallel", "arbitrary")),
        name="mixer_in",
    )(x, mod, norm_g, w_in, qk_g, bd, ln, ws, bias)


def _attn_kernel(q_ref, k_ref, v_ref, o_ref, acc_ref, carry_ref):
    i = pl.program_id(2)
    tq = ATT_TILE
    hb = SEQ_BLOCK
    n_sub = tq // hb
    n_pairs = q_ref.shape[2] // PAIR
    first = lax.broadcasted_iota(jnp.int32, (tq, PAIR), 1) < HEAD_DIM
    zero = jnp.zeros((tq, PAIR), BF16)
    q2 = []
    for p in range(n_pairs):
        q = q_ref[0, :, p * PAIR:(p + 1) * PAIR]
        q2.append(jnp.concatenate([jnp.where(first, q, zero), jnp.where(first, zero, q)], axis=0))

    kk = lax.broadcasted_iota(jnp.int32, (2 * hb, 2 * hb), 0) & (hb - 1)
    nn = lax.broadcasted_iota(jnp.int32, (2 * hb, 2 * hb), 1)
    csum = jnp.where((kk > nn) | (nn >= hb), -1.0, 0.0).astype(BF16)

    qpos = lax.broadcasted_iota(jnp.int32, (2 * tq, tq), 0) & (tq - 1)
    kpos = lax.broadcasted_iota(jnp.int32, (2 * tq, tq), 1)
    strict = kpos < qpos

    def tile(j, diagonal):
        for p in range(n_pairs):
            pair_tile(j, diagonal, p)

    def pair_tile(j, diagonal, p):
        start = pl.multiple_of(j * tq, tq)
        lanes = slice(p * PAIR, (p + 1) * PAIR)
        kb = k_ref[0, pl.ds(start, tq), lanes]
        vb = v_ref[0, pl.ds(start, tq), lanes]
        z = lax.dot_general(q2[p], kb, (((1,), (1,)), ((), ())), preferred_element_type=F32)
        sp = jnp.maximum(z, 0.0) + jnp.log(1.0 + jnp.exp2(jnp.abs(z) * (-LOG2_E)))
        if diagonal:
            sp = jnp.where(strict, sp, 0.0)
        hi, lo = _split_bf16(sp)
        carry = None if diagonal else carry_ref[p]
        la = [None] * n_sub
        for sb in reversed(range(n_sub)):
            cols = slice(sb * hb, (sb + 1) * hb)
            lhs = jnp.concatenate([hi[:, cols], lo[:, cols]], axis=1)
            cs = jnp.dot(lhs, csum, preferred_element_type=F32)
            la[sb] = cs[:, :hb] if carry is None else cs[:, :hb] + carry
            carry = cs[:, hb:] if carry is None else carry + cs[:, hb:]
        carry_ref[p] = carry
        w = jnp.exp(z - sp + jnp.concatenate(la, axis=1))
        if diagonal:
            w = jnp.where(strict, w, 0.0)
        pv = jnp.dot(w.astype(BF16), vb, preferred_element_type=F32)
        if diagonal:
            acc_ref[p] = pv
        else:
            acc_ref[p] += pv

    tile(i, True)

    def body(t, c):
        tile(i - 1 - t, False)
        return c

    lax.fori_loop(0, i, body, 0)
    for p in range(n_pairs):
        o_ref[0, :, p * PAIR:(p + 1) * PAIR] = jnp.where(
            first, acc_ref[p, 0:tq, :], acc_ref[p, tq:2 * tq, :]).astype(BF16)


def _attn_call(q, k, v):
    b, s, w = q.shape
    tq = ATT_TILE
    wl = ATT_PAIRS * PAIR if w % (ATT_PAIRS * PAIR) == 0 else PAIR
    assert s % tq == 0
    return pl.pallas_call(
        _attn_kernel,
        grid=(b, w // wl, s // tq),
        in_specs=[
            pl.BlockSpec((1, tq, wl), lambda bi, p, i: (bi, i, p)),
            pl.BlockSpec((1, s, wl), lambda bi, p, i: (bi, 0, p)),
            pl.BlockSpec((1, s, wl), lambda bi, p, i: (bi, 0, p)),
        ],
        out_specs=pl.BlockSpec((1, tq, wl), lambda bi, p, i: (bi, i, p)),
        out_shape=jax.ShapeDtypeStruct((b, s, w), BF16),
        scratch_shapes=[pltpu.VMEM((wl // PAIR, 2 * tq, PAIR), F32),
                        pltpu.VMEM((wl // PAIR, 2 * tq, SEQ_BLOCK), F32)],
        compiler_params=_cparams(("parallel", "parallel", "arbitrary")),
        name="sb_attn",
    )(q, k, v)


def _outproj_kernel(*refs, sbw, with_router):
    if with_router:
        oa_ref, ob_ref, x_ref, mod_ref, ng_ref, w_ref, rw_ref, rb_ref, x1_ref, h_ref, lg_ref = refs
    else:
        oa_ref, ob_ref, x_ref, mod_ref, ng_ref, w_ref, x1_ref, h_ref = refs
    y = jnp.dot(oa_ref[0], w_ref[0:sbw, :], preferred_element_type=F32)
    y = y + jnp.dot(ob_ref[0], w_ref[sbw:, :], preferred_element_type=F32)
    x1 = x_ref[0] + mod_ref[0, 0:1, :] * y
    x1_ref[0] = x1
    ms = jnp.mean(x1 * x1, axis=-1, keepdims=True)
    hn = x1 * lax.rsqrt(ms + EPS) * ng_ref[...]
    h = (hn * (1.0 + mod_ref[0, 2:3, :]) + mod_ref[0, 1:2, :]).astype(BF16)
    h_ref[0] = h
    if with_router:
        lg = lax.dot_general(rw_ref[...], h, (((1,), (1,)), ((), ())), preferred_element_type=F32)
        lg_ref[...] = lg[0:N_EXPERTS, :] + rb_ref[...]


def _outproj_call(oa, ob, x, mod, norm_g, w_out, router=None):
    b, s, d = x.shape
    sbw = oa.shape[-1]
    tm = min(TM_PROJ, s)
    row_spec = lambda w: pl.BlockSpec((1, tm, w), lambda i, j: (i, j, 0))
    const = lambda shape: pl.BlockSpec(shape, lambda i, j: (0,) * len(shape))
    in_specs = [row_spec(sbw), row_spec(ob.shape[-1]), row_spec(d),
                pl.BlockSpec((1, 3, d), lambda i, j: (i, 0, 0)), const((1, d)), const(w_out.shape)]
    out_specs = [row_spec(d), row_spec(d)]
    out_shape = [jax.ShapeDtypeStruct((b, s, d), F32), jax.ShapeDtypeStruct((b, s, d), BF16)]
    args = [oa, ob, x, mod, norm_g, w_out]
    if router is not None:
        rw_t, rb = router
        in_specs += [const(rw_t.shape), const(rb.shape)]
        nj = s // tm
        out_specs.append(pl.BlockSpec((N_EXPERTS, tm), lambda i, j: (0, i * nj + j)))
        out_shape.append(jax.ShapeDtypeStruct((N_EXPERTS, b * s), F32))
        args += [rw_t, rb]
    return pl.pallas_call(
        functools.partial(_outproj_kernel, sbw=sbw, with_router=router is not None),
        grid=(b, s // tm),
        in_specs=in_specs,
        out_specs=out_specs,
        out_shape=out_shape,
        compiler_params=_cparams(("parallel", "arbitrary")),
        name="mixer_out",
    )(*args)


def _ffn_kernel(h_ref, x_ref, g2_ref, wg_ref, wu_ref, wd_ref, o_ref, acc_ref):
    h = h_ref[0]
    n_chunks = wg_ref.shape[0]

    def body(c, carry):
        g = jnp.dot(h, wg_ref[c], preferred_element_type=F32)
        u = jnp.dot(h, wu_ref[c], preferred_element_type=F32)
        a = (g * jax.nn.sigmoid(g) * u).astype(BF16)
        acc_ref[...] += jnp.dot(a, wd_ref[c], preferred_element_type=F32)
        return carry

    acc_ref[...] = jnp.zeros_like(acc_ref)
    lax.fori_loop(0, n_chunks, body, 0)
    o_ref[0] = x_ref[0] + g2_ref[0] * acc_ref[...]


def _ffn_call(h, x, g2, wg, wu, wd):
    b, s, d = x.shape
    tm = min(TM_FFN, s)
    row_spec = pl.BlockSpec((1, tm, d), lambda i, j: (i, j, 0))
    resident = lambda a: pl.BlockSpec(a.shape, lambda i, j: (0, 0, 0), pipeline_mode=pl.Buffered(1))
    return pl.pallas_call(
        _ffn_kernel,
        grid=(b, s // tm),
        in_specs=[row_spec, row_spec, pl.BlockSpec((1, 1, d), lambda i, j: (i, 0, 0)),
                  resident(wg), resident(wu), resident(wd)],
        out_specs=row_spec,
        out_shape=jax.ShapeDtypeStruct((b, s, d), F32),
        scratch_shapes=[pltpu.VMEM((tm, d), F32)],
        compiler_params=_cparams(("parallel", "arbitrary")),
        name="dense_ffn",
    )(h, x, g2, wg, wu, wd)


def _route_kernel(lg_ref, oi_ref, of_ref, tile_ref, cnt_ref, carry_ref):
    step = pl.program_id(0)
    tl = lg_ref.shape[1]
    ne = N_EXPERTS

    @pl.when(step == 0)
    def _():
        carry_ref[...] = jnp.zeros_like(carry_ref)

    lg = lg_ref[...]
    eidx = lax.broadcasted_iota(jnp.int32, (ne, tl), 0).astype(F32)
    m1 = jnp.max(lg, axis=0, keepdims=True)
    i1 = jnp.min(jnp.where(lg == m1, eidx, float(ne)), axis=0, keepdims=True)
    lg2 = jnp.where(eidx == i1, -jnp.inf, lg)
    m2 = jnp.max(lg2, axis=0, keepdims=True)
    i2 = jnp.min(jnp.where(lg2 == m2, eidx, float(ne)), axis=0, keepdims=True)
    t = jnp.exp(m2 - m1)
    gate1 = 1.0 / (1.0 + t)
    gate2 = t / (1.0 + t)

    sel1 = eidx == i1
    sel2 = eidx == i2
    onehot = jnp.where(sel1 | sel2, 1.0, 0.0)
    a = lax.broadcasted_iota(jnp.int32, (tl, tl), 0)
    bcol = lax.broadcasted_iota(jnp.int32, (tl, tl), 1)
    upper = jnp.where(a < bcol, 1.0, 0.0).astype(BF16)
    lhs = jnp.concatenate([onehot, jnp.zeros_like(onehot)], axis=0).astype(BF16)
    excl_tile = jnp.dot(lhs, upper, preferred_element_type=F32)[0:ne, :]
    before = carry_ref[...]
    n_tile = jnp.sum(onehot, axis=1, keepdims=True) + jnp.zeros_like(before)
    n_tile = jnp.floor((n_tile + (SUBLANES - 1)) * (1.0 / SUBLANES)) * SUBLANES
    off = jnp.concatenate([jnp.zeros((1, LANES), F32)] +
                          [jnp.sum(n_tile[0:e], axis=0, keepdims=True) for e in range(1, ne)], axis=0)
    pick = lambda sel, v: jnp.sum(jnp.where(sel, v, 0.0), axis=0, keepdims=True)
    r1 = pick(sel1, excl_tile + before[:, 0:1])
    r2 = pick(sel2, excl_tile + before[:, 0:1])
    p1 = pick(sel1, excl_tile + off[:, 0:1])
    p2 = pick(sel2, excl_tile + off[:, 0:1])
    carry_ref[...] = before + n_tile

    as_i = lambda v: v.astype(jnp.int32)
    oi_ref[...] = jnp.concatenate(
        [as_i(i1), as_i(i2), as_i(r1), as_i(r2), as_i(p1), as_i(p2), jnp.zeros((ne - 6, tl), jnp.int32)],
        axis=0)
    of_ref[...] = jnp.concatenate([gate1, gate2, jnp.zeros((ne - 2, tl), F32)], axis=0)
    lane = lax.broadcasted_iota(jnp.int32, (ne, LANES), 1)
    tile_ref[0] = as_i(jnp.where(lane == 0, n_tile, jnp.where(lane == 1, before, 0.0)))
    cnt_ref[...] = carry_ref[...]


def _route_call(logits_t):
    ne, t = logits_t.shape
    tl = min(TT_ROWS, t)
    blk = pl.BlockSpec((ne, tl), lambda i: (0, i))
    return pl.pallas_call(
        _route_kernel,
        grid=(t // tl,),
        in_specs=[blk],
        out_specs=[blk, blk, pl.BlockSpec((1, ne, LANES), lambda i: (i, 0, 0)),
                   pl.BlockSpec((ne, LANES), lambda i: (0, 0))],
        out_shape=[jax.ShapeDtypeStruct((ne, t), jnp.int32), jax.ShapeDtypeStruct((ne, t), F32),
                   jax.ShapeDtypeStruct((t // tl, ne, LANES), jnp.int32),
                   jax.ShapeDtypeStruct((ne, LANES), F32)],
        scratch_shapes=[pltpu.VMEM((ne, LANES), F32)],
        compiler_params=_cparams(("arbitrary",)),
        name="moe_route",
    )(logits_t)


def _run_pieces(n, max_rows):
    size = max_rows
    while size >= SUBLANES:
        yield size, pl.multiple_of(n & ~(2 * size - 1), SUBLANES), (n & size) != 0
        size //= 2


def _compact_rows(tt):
    return 2 * tt + N_EXPERTS * SUBLANES


def _dispatch_kernel(tail_ref, cnt_ref, base_ref, pos_ref, h_ref, xb_ref, c_ref, zero_ref, sem, zsem):
    step = pl.program_id(0)
    tt = h_ref.shape[0]
    tmb = zero_ref.shape[0]

    def zero_block(blk):
        row = pl.multiple_of(blk * tmb, tmb)
        return pltpu.make_async_copy(zero_ref, xb_ref.at[pl.ds(row, tmb)], zsem)

    @pl.when(step == 0)
    def _():
        zero_ref[...] = jnp.zeros_like(zero_ref)
        for e in range(N_EXPERTS):
            zero_block(tail_ref[e]).start()
        for e in range(N_EXPERTS):
            zero_block(tail_ref[e]).wait()

        def zero_unused(blk, c):
            cp = zero_block(blk)
            cp.start()
            cp.wait()
            return c

        lax.fori_loop(tail_ref[N_EXPERTS], xb_ref.shape[0] // tmb, zero_unused, 0)

    r = lax.broadcasted_iota(jnp.int32, (c_ref.shape[0], tt), 0)
    onehot = jnp.where((r == pos_ref[4:5, :]) | (r == pos_ref[5:6, :]), 1.0, 0.0).astype(BF16)
    c_ref[...] = jnp.dot(onehot, h_ref[...], preferred_element_type=F32)

    def run_copies(op):
        off = 0
        for e in range(N_EXPERTS):
            n = cnt_ref[step * N_EXPERTS + e]
            dst = pl.multiple_of(base_ref[step * N_EXPERTS + e], SUBLANES)
            for size, piece, present in _run_pieces(n, tt):
                @pl.when(present)
                def _():
                    src = pl.multiple_of(off + piece, SUBLANES)
                    op(pltpu.make_async_copy(c_ref.at[pl.ds(src, size)],
                                             xb_ref.at[pl.ds(pl.multiple_of(dst + piece, SUBLANES), size)],
                                             sem))
            off = off + n

    run_copies(lambda cp: cp.start())
    run_copies(lambda cp: cp.wait())


def _dispatch_call(tail_blocks, tile_cnt, tile_base, pos, h, n_slots):
    t, d = h.shape
    tt = min(TT_ROWS, t)
    tmb = TM_MOE
    return pl.pallas_call(
        _dispatch_kernel,
        grid_spec=pltpu.PrefetchScalarGridSpec(
            num_scalar_prefetch=3,
            grid=(t // tt,),
            in_specs=[pl.BlockSpec((N_EXPERTS, tt), lambda i, *_: (0, i)),
                      pl.BlockSpec((tt, d), lambda i, *_: (i, 0))],
            out_specs=pl.BlockSpec(memory_space=pl.ANY),
            scratch_shapes=[pltpu.VMEM((_compact_rows(tt), d), F32), pltpu.VMEM((tmb, d), F32),
                            pltpu.SemaphoreType.DMA, pltpu.SemaphoreType.DMA],
        ),
        out_shape=jax.ShapeDtypeStruct((n_slots, d), F32),
        compiler_params=_cparams(("arbitrary",)),
        name="moe_dispatch",
    )(tail_blocks, tile_cnt, tile_base, pos, h)


def _moe_ffn_kernel(be_ref, nb_ref, x_ref, wg_ref, wu_ref, wd_ref, o_ref, acc_ref, xb_ref):
    i = pl.program_id(0)
    c = pl.program_id(1)

    @pl.when(i < nb_ref[0])
    def _():
        @pl.when(c == 0)
        def _():
            acc_ref[...] = jnp.zeros_like(acc_ref)
            xb_ref[...] = x_ref[...].astype(BF16)

        xb = xb_ref[...]
        g = jnp.dot(xb, wg_ref[0], preferred_element_type=F32)
        u = jnp.dot(xb, wu_ref[0], preferred_element_type=F32)
        a = (g * jax.nn.sigmoid(g) * u).astype(BF16)
        acc_ref[...] += jnp.dot(a, wd_ref[0], preferred_element_type=F32)

        @pl.when(c == pl.num_programs(1) - 1)
        def _():
            o_ref[...] = acc_ref[...]

    @pl.when((i >= nb_ref[0]) & (c == pl.num_programs(1) - 1))
    def _():
        o_ref[...] = jnp.zeros_like(o_ref)


def _moe_ffn_call(block_e, n_used, xb, wg, wu, wd):
    n_slots, d = xb.shape
    tmb = TM_MOE
    nb = n_slots // tmb
    f = wg.shape[2]
    tf = _largest_divisor(f, (512, 256, 128))
    nc = f // tf
    blk = lambda i, nbu: jnp.minimum(i, nbu[0] - 1)
    chunk = lambda i, c, nbu: jnp.where(i < nbu[0], c, nc - 1)
    return pl.pallas_call(
        _moe_ffn_kernel,
        grid_spec=pltpu.PrefetchScalarGridSpec(
            num_scalar_prefetch=2,
            grid=(nb, nc),
            in_specs=[
                pl.BlockSpec((tmb, d), lambda i, c, be, nbu: (blk(i, nbu), 0)),
                pl.BlockSpec((1, d, tf), lambda i, c, be, nbu: (be[blk(i, nbu)], 0, chunk(i, c, nbu))),
                pl.BlockSpec((1, d, tf), lambda i, c, be, nbu: (be[blk(i, nbu)], 0, chunk(i, c, nbu))),
                pl.BlockSpec((1, tf, d), lambda i, c, be, nbu: (be[blk(i, nbu)], chunk(i, c, nbu), 0)),
            ],
            out_specs=pl.BlockSpec((tmb, d), lambda i, c, be, nbu: (i, 0)),
            scratch_shapes=[pltpu.VMEM((tmb, d), F32), pltpu.VMEM((tmb, d), BF16)],
        ),
        out_shape=jax.ShapeDtypeStruct((n_slots, d), F32),
        compiler_params=_cparams(("arbitrary", "arbitrary")),
        name="moe_ffn",
    )(block_e, n_used, xb, wg, wu, wd)


def _combine_kernel(dest_ref, yb_ref, x_ref, g2_ref, gate_ref, o_ref, rows_ref, sem):
    tt = x_ref.shape[1]

    def row_copy(t, k):
        return pltpu.make_async_copy(yb_ref.at[pl.ds(dest_ref[0, k, t], 1)],
                                     rows_ref.at[k, pl.ds(t, 1)], sem)

    def start(t, c):
        row_copy(t, 0).start()
        row_copy(t, 1).start()
        return c

    def wait(t, c):
        row_copy(t, 0).wait()
        row_copy(t, 1).wait()
        return c

    lax.fori_loop(0, tt, start, 0, unroll=8)
    lax.fori_loop(0, tt, wait, 0, unroll=8)
    y = rows_ref[0] * gate_ref[0, :, 0:1] + rows_ref[1] * gate_ref[0, :, 1:2]
    o_ref[0] = x_ref[0] + g2_ref[0] * y


def _combine_call(dest, yb, x, g2, gates):
    b, s, d = x.shape
    tt = dest.shape[2]
    nj = s // tt
    row_spec = pl.BlockSpec((1, tt, d), lambda i, j: (i, j, 0))
    return pl.pallas_call(
        _combine_kernel,
        grid=(b, nj),
        in_specs=[
            pl.BlockSpec((1, 2, tt), lambda i, j: (i * nj + j, 0, 0), memory_space=pltpu.SMEM),
            pl.BlockSpec(memory_space=pl.ANY),
            row_spec,
            pl.BlockSpec((1, 1, d), lambda i, j: (i, 0, 0)),
            pl.BlockSpec((1, tt, 2), lambda i, j: (i * nj + j, 0, 0)),
        ],
        out_specs=row_spec,
        out_shape=jax.ShapeDtypeStruct((b, s, d), F32),
        scratch_shapes=[pltpu.VMEM((2, tt, d), F32), pltpu.SemaphoreType.DMA],
        compiler_params=_cparams(("arbitrary", "arbitrary")),
        name="moe_combine",
    )(dest, yb, x, g2, gates)


def _chunk_cols(w, tf):
    d, f = w.shape
    return w.reshape(d, f // tf, tf).transpose(1, 0, 2).astype(BF16)


def _moe_layer(h, x1, g2, logits_t, wg, wu, wd):
    b, s, d = x1.shape
    t = b * s
    tmb = TM_MOE
    tt = min(TT_ROWS, s)
    ne = N_EXPERTS
    oi, of, tiles, cnt = _route_call(logits_t)
    e1, e2, r1, r2 = oi[0], oi[1], oi[2], oi[3]
    counts = cnt[:, 0].astype(jnp.int32)
    padded = (counts + tmb - 1) // tmb * tmb
    pad_end = jnp.cumsum(padded)
    pad_start = pad_end - padded
    n_used = (pad_end[-1] // tmb).astype(jnp.int32)
    nb = -(-(2 * t + (t // tt) * ne * (SUBLANES - 1)) // tmb) + ne
    block_row0 = jnp.arange(nb, dtype=jnp.int32) * tmb
    block_e = jnp.minimum(jnp.sum(pad_end[None, :] <= block_row0[:, None], axis=1), ne - 1).astype(jnp.int32)
    tail_blocks = jnp.where(padded > 0, pad_end // tmb - 1, nb - 1)
    tail_blocks = jnp.concatenate([tail_blocks, n_used[None]]).astype(jnp.int32)
    start_of = lambda e: jnp.sum(jnp.where(e[None, :] == jnp.arange(ne)[:, None], pad_start[:, None], 0), axis=0)
    d1 = start_of(e1) + r1
    d2 = start_of(e2) + r2
    dest = jnp.stack([d1.reshape(t // tt, tt), d2.reshape(t // tt, tt)], axis=1).astype(jnp.int32)

    tile_cnt = tiles[:, :, 0].reshape(-1)
    tile_base = (tiles[:, :, 1] + pad_start[None, :]).reshape(-1).astype(jnp.int32)
    xb = _dispatch_call(tail_blocks, tile_cnt, tile_base, oi, h.reshape(t, d), nb * tmb)
    yb = _moe_ffn_call(block_e, n_used.reshape(1), xb, wg, wu, wd)
    gates = jnp.stack([of[0], of[1]], axis=-1).reshape(t // tt, tt, 2)
    return _combine_call(dest, yb, x1, g2, gates)


def kernel(x, c, ada_w, ada_b, norm_mix_g, norm_ffn_g, w_in, q_norm_g, k_norm_g, sg_ln_g, sg_ln_b,
           sg_w_spatial, sg_b_spatial, w_out, ffn_w_gate, ffn_w_up, ffn_w_down, router_w, router_b,
           moe_w_gate, moe_w_up, moe_w_down):
    b, s, d = x.shape
    depth = ada_w.shape[0]
    sgw = sg_ln_g.shape[-1]
    sbw = w_out.shape[1] - sgw
    assert sbw % PAIR == 0 and sgw % PAIR == 0 and s % SEQ_BLOCK == 0
    assert w_in.shape[-1] == 3 * sbw + 2 * sgw

    ada = _ada_call(c, ada_w, ada_b)
    ada = ada.reshape(depth, b, 6, d)

    wb = MXU_DIM if sbw % MXU_DIM == 0 else PAIR
    gi = jnp.arange(wb) // HEAD_DIM
    bd = (gi[:, None] == gi[None, :]).astype(BF16)
    pos = jnp.arange(SEQ_BLOCK)
    chunk_causal = (pos[None, :] // CHUNK) <= (pos[:, None] // CHUNK)
    scale = HEAD_DIM ** -0.5

    for l in range(depth):
        sh1, sc1, g1, sh2, sc2, g2 = (ada[l, :, j] for j in range(6))
        qk_g = jnp.stack([jnp.tile(q_norm_g[l], sbw // HEAD_DIM) * scale,
                          jnp.tile(k_norm_g[l], sbw // HEAD_DIM)])
        ln = jnp.stack([sg_ln_g[l], sg_ln_b[l]])
        ws = jnp.where(chunk_causal[None], sg_w_spatial[l], 0.0)
        ws = ws.reshape(sgw // PAIR, 2, SEQ_BLOCK, SEQ_BLOCK).transpose(0, 2, 1, 3)
        ws = ws.reshape(sgw // PAIR, SEQ_BLOCK, 2 * SEQ_BLOCK).astype(BF16)
        bias = jnp.repeat(sg_b_spatial[l].T, HEAD_DIM, axis=1)

        q, k, v, ob = _inproj_call(x, jnp.stack([sh1, sc1], axis=1), norm_mix_g[l][None],
                                   w_in[l].astype(BF16), qk_g, bd, ln, ws, bias, sbw=sbw, sgw=sgw)
        oa = _attn_call(q, k, v)
        mod2 = jnp.stack([g1, sh2, sc2], axis=1)
        j = l // 2
        if l % 2 == 0:
            x1, h = _outproj_call(oa, ob, x, mod2, norm_ffn_g[l][None], w_out[l].astype(BF16))
            tf = _largest_divisor(ffn_w_gate.shape[-1], (512, 256, 128))
            wg = _chunk_cols(ffn_w_gate[j], tf)
            wu = _chunk_cols(ffn_w_up[j], tf)
            wd = ffn_w_down[j].reshape(-1, tf, d).astype(BF16)
            x = _ffn_call(h, x1, g2[:, None], wg, wu, wd)
        else:
            rw_t = jnp.zeros((2 * N_EXPERTS, d), BF16).at[:N_EXPERTS].set(router_w[j].T.astype(BF16))
            rb = router_b[j].reshape(N_EXPERTS, 1)
            x1, h, logits_t = _outproj_call(oa, ob, x, mod2, norm_ffn_g[l][None],
                                            w_out[l].astype(BF16), router=(rw_t, rb))
            x = _moe_layer(h, x1, g2[:, None], logits_t, moe_w_gate[j].astype(BF16),
                           moe_w_up[j].astype(BF16), moe_w_down[j].astype(BF16))
    return x
```

```python
import functools
import math

import jax
import jax.numpy as jnp
from jax import lax
from jax.experimental import pallas as pl
from jax.experimental.pallas import tpu as pltpu

F32 = jnp.float32
BF16 = jnp.bfloat16

HEAD_DIM = 64
PAIR = 2 * HEAD_DIM
SEQ_BLOCK = 128
CHUNK = 64
N_EXPERTS = 8
EPS = 1e-6
LOG2_E = 1.4426950408889634

LANES = 128
SUBLANES = 8
MXU_DIM = 256
VMEM_LIMIT_BYTES = 56 * 1024 * 1024

ATT_TILE = MXU_DIM
ATT_PAIRS = 4
ATT_KEY_TILES = 1
TM_PROJ = 512
TM_FFN = 512
TM_MOE = 1024
FFN_SUB = 2
TT_ROWS = 512


def _cparams(sem):
    return pltpu.CompilerParams(dimension_semantics=sem, vmem_limit_bytes=VMEM_LIMIT_BYTES)


def _split_bf16(x):
    hi = x.astype(BF16)
    lo = (x - hi.astype(F32)).astype(BF16)
    return hi, lo


def _largest_divisor(n, candidates):
    for c in candidates:
        if n % c == 0:
            return c
    raise ValueError(f"no tile in {candidates} divides {n}")


def _ada_kernel(c_ref, w_ref, b_ref, o_ref):
    c = c_ref[...]
    c_act = (c * jax.nn.sigmoid(c)).astype(BF16)
    o_ref[0] = jnp.dot(c_act, w_ref[0].astype(BF16), preferred_element_type=F32) + b_ref[0]


def _ada_call(c, ada_w, ada_b):
    depth, d, n = ada_w.shape
    b = c.shape[0]
    tn = _largest_divisor(n, (1536, 1024, 512, 256, 128))
    return pl.pallas_call(
        _ada_kernel,
        grid=(depth, n // tn),
        in_specs=[
            pl.BlockSpec((b, d), lambda l, j: (0, 0)),
            pl.BlockSpec((1, d, tn), lambda l, j: (l, 0, j)),
            pl.BlockSpec((1, 1, tn), lambda l, j: (l, 0, j)),
        ],
        out_specs=pl.BlockSpec((1, b, tn), lambda l, j: (l, 0, j)),
        out_shape=jax.ShapeDtypeStruct((depth, b, n), F32),
        compiler_params=_cparams(("arbitrary", "arbitrary")),
        name="ada_ln",
    )(c, ada_w, ada_b.reshape(depth, 1, n))


def _group_mean_sq(t, bd):
    wb = bd.shape[0]
    sq = t * t
    hi, lo = _split_bf16(sq)
    cols = []
    for c0 in range(0, t.shape[1], wb):
        s = jnp.dot(hi[:, c0:c0 + wb], bd, preferred_element_type=F32)
        s = s + jnp.dot(lo[:, c0:c0 + wb], bd, preferred_element_type=F32)
        cols.append(s)
    ss = cols[0] if len(cols) == 1 else jnp.concatenate(cols, axis=1)
    return ss * (1.0 / HEAD_DIM)


def _gelu(t):
    return 0.5 * t * (1.0 + lax.erf(t * math.sqrt(0.5)))


def _inproj_kernel(x_ref, mod_ref, ng_ref, w_ref, qkg_ref, bd_ref, ln_ref, ws_ref, bias_ref,
                   q_ref, k_ref, v_ref, ob_ref, *, sbw, sgw):
    xf = x_ref[0]
    ms = jnp.mean(xf * xf, axis=-1, keepdims=True)
    y = xf * lax.rsqrt(ms + EPS) * ng_ref[...]
    h = (y * (1.0 + mod_ref[0, 1:2, :]) + mod_ref[0, 0:1, :]).astype(BF16)

    bd = bd_ref[...]
    q = jnp.dot(h, w_ref[:, 0:sbw], preferred_element_type=F32)
    qn = q * lax.rsqrt(_group_mean_sq(q, bd) + EPS) * qkg_ref[0:1, :]
    q_ref[0] = qn.astype(BF16)
    k = jnp.dot(h, w_ref[:, sbw:2 * sbw], preferred_element_type=F32)
    kn = k * lax.rsqrt(_group_mean_sq(k, bd) + EPS) * qkg_ref[1:2, :]
    k_ref[0] = kn.astype(BF16)
    v_ref[0] = jnp.dot(h, w_ref[:, 2 * sbw:3 * sbw], preferred_element_type=F32).astype(BF16)

    u = _gelu(jnp.dot(h, w_ref[:, 3 * sbw:3 * sbw + sgw], preferred_element_type=F32))
    g = _gelu(jnp.dot(h, w_ref[:, 3 * sbw + sgw:3 * sbw + 2 * sgw], preferred_element_type=F32))
    mu = jnp.mean(g, axis=-1, keepdims=True)
    gc = g - mu
    var = jnp.mean(gc * gc, axis=-1, keepdims=True)
    gn = (gc * lax.rsqrt(var + EPS) * ln_ref[0:1, :] + ln_ref[1:2, :]).astype(BF16)

    tm = xf.shape[0]
    first = lax.broadcasted_iota(jnp.int32, (SEQ_BLOCK, PAIR), 1) < HEAD_DIM
    zero = jnp.zeros((SEQ_BLOCK, PAIR), BF16)
    for r in range(tm // SEQ_BLOCK):
        rows = slice(r * SEQ_BLOCK, (r + 1) * SEQ_BLOCK)
        for p in range(sgw // PAIR):
            cols = slice(p * PAIR, (p + 1) * PAIR)
            vb = gn[rows, cols]
            rhs = jnp.concatenate([jnp.where(first, vb, zero), jnp.where(first, zero, vb)], axis=0)
            mixed = jnp.dot(ws_ref[p], rhs, preferred_element_type=F32)
            ob_ref[0, rows, cols] = (u[rows, cols] * (mixed + bias_ref[:, cols])).astype(BF16)


def _inproj_call(x, mod, norm_g, w_in, qk_g, bd, ln, ws, bias, *, sbw, sgw):
    b, s, d = x.shape
    tm = min(TM_PROJ, s)
    n_in = w_in.shape[1]
    out = jax.ShapeDtypeStruct((b, s, sbw), BF16)
    row_spec = lambda w: pl.BlockSpec((1, tm, w), lambda i, j: (i, j, 0))
    const = lambda shape: pl.BlockSpec(shape, lambda i, j: (0,) * len(shape))
    return pl.pallas_call(
        functools.partial(_inproj_kernel, sbw=sbw, sgw=sgw),
        grid=(b, s // tm),
        in_specs=[
            row_spec(d),
            pl.BlockSpec((1, 2, d), lambda i, j: (i, 0, 0)),
            const((1, d)),
            const((d, n_in)),
            const((2, sbw)),
            const(bd.shape),
            const((2, sgw)),
            const(ws.shape),
            const((SEQ_BLOCK, sgw)),
        ],
        out_specs=[row_spec(sbw), row_spec(sbw), row_spec(sbw), row_spec(sgw)],
        out_shape=[out, out, out, jax.ShapeDtypeStruct((b, s, sgw), BF16)],
        compiler_params=_cparams(("parallel", "arbitrary")),
        name="mixer_in",
    )(x, mod, norm_g, w_in, qk_g, bd, ln, ws, bias)


def _attn_kernel(q_ref, k_ref, v_ref, o_ref, acc_ref, carry_ref):
    i = pl.program_id(2)
    tq = ATT_TILE
    hb = SEQ_BLOCK
    n_sub = tq // hb
    n_pairs = q_ref.shape[2] // PAIR
    first = lax.broadcasted_iota(jnp.int32, (tq, PAIR), 1) < HEAD_DIM
    zero = jnp.zeros((tq, PAIR), BF16)
    q2 = []
    for p in range(n_pairs):
        q = q_ref[0, :, p * PAIR:(p + 1) * PAIR]
        q2.append(jnp.concatenate([jnp.where(first, q, zero), jnp.where(first, zero, q)], axis=0))

    kk = lax.broadcasted_iota(jnp.int32, (2 * hb, 2 * hb), 0) & (hb - 1)
    nn = lax.broadcasted_iota(jnp.int32, (2 * hb, 2 * hb), 1)
    csum = jnp.where((kk >= nn) | (nn >= hb), -1.0, 0.0).astype(BF16)

    qpos = lax.broadcasted_iota(jnp.int32, (2 * tq, tq), 0) & (tq - 1)
    kpos = lax.broadcasted_iota(jnp.int32, (2 * tq, tq), 1)
    strict = kpos < qpos

    def weights(j, p, carry, diagonal):
        start = pl.multiple_of(j * tq, tq)
        kb = k_ref[0, pl.ds(start, tq), p * PAIR:(p + 1) * PAIR]
        z = lax.dot_general(q2[p], kb, (((1,), (1,)), ((), ())), preferred_element_type=F32)
        sp = jnp.maximum(z, 0.0) + jnp.log(1.0 + jnp.exp2(jnp.abs(z) * (-LOG2_E)))
        if diagonal:
            sp = jnp.where(strict, sp, 0.0)
        hi, lo = _split_bf16(sp)
        la = [None] * n_sub
        for sb in reversed(range(n_sub)):
            cols = slice(sb * hb, (sb + 1) * hb)
            lhs = jnp.concatenate([hi[:, cols], lo[:, cols]], axis=1)
            cs = jnp.dot(lhs, csum, preferred_element_type=F32)
            la[sb] = cs[:, :hb] if carry is None else cs[:, :hb] + carry
            carry = cs[:, hb:] if carry is None else carry + cs[:, hb:]
        w = jnp.exp(z + jnp.concatenate(la, axis=1))
        if diagonal:
            w = jnp.where(strict, w, 0.0)
        return w.astype(BF16), carry

    def sweep(j_hi, n_tiles, diagonal):
        start = pl.multiple_of((j_hi - (n_tiles - 1)) * tq, tq)
        for p in range(n_pairs):
            carry = None if diagonal else carry_ref[p]
            ws = []
            for n in range(n_tiles):
                w, carry = weights(j_hi - n, p, carry, diagonal and n == 0)
                ws.insert(0, w)
            carry_ref[p] = carry
            wcat = ws[0] if n_tiles == 1 else jnp.concatenate(ws, axis=1)
            vb = v_ref[0, pl.ds(start, n_tiles * tq), p * PAIR:(p + 1) * PAIR]
            pv = jnp.dot(wcat, vb, preferred_element_type=F32)
            if diagonal:
                acc_ref[p] = pv
            else:
                acc_ref[p] += pv

    sweep(i, 1, True)

    def body(t, c):
        sweep(i - 1 - ATT_KEY_TILES * t, ATT_KEY_TILES, False)
        return c

    lax.fori_loop(0, i // ATT_KEY_TILES, body, 0)
    for rem in range(1, ATT_KEY_TILES):
        @pl.when(i % ATT_KEY_TILES == rem)
        def _():
            sweep(rem - 1, rem, False)
    for p in range(n_pairs):
        o_ref[0, :, p * PAIR:(p + 1) * PAIR] = jnp.where(
            first, acc_ref[p, 0:tq, :], acc_ref[p, tq:2 * tq, :]).astype(BF16)


def _attn_call(q, k, v):
    b, s, w = q.shape
    tq = ATT_TILE
    wl = ATT_PAIRS * PAIR if w % (ATT_PAIRS * PAIR) == 0 else PAIR
    assert s % tq == 0
    return pl.pallas_call(
        _attn_kernel,
        grid=(b, w // wl, s // tq),
        in_specs=[
            pl.BlockSpec((1, tq, wl), lambda bi, p, i: (bi, i, p)),
            pl.BlockSpec((1, s, wl), lambda bi, p, i: (bi, 0, p)),
            pl.BlockSpec((1, s, wl), lambda bi, p, i: (bi, 0, p)),
        ],
        out_specs=pl.BlockSpec((1, tq, wl), lambda bi, p, i: (bi, i, p)),
        out_shape=jax.ShapeDtypeStruct((b, s, w), BF16),
        scratch_shapes=[pltpu.VMEM((wl // PAIR, 2 * tq, PAIR), F32),
                        pltpu.VMEM((wl // PAIR, 2 * tq, SEQ_BLOCK), F32)],
        compiler_params=_cparams(("parallel", "parallel", "arbitrary")),
        name="sb_attn",
    )(q, k, v)


def _outproj_kernel(*refs, sbw, with_router):
    if with_router:
        oa_ref, ob_ref, x_ref, mod_ref, ng_ref, w_ref, rw_ref, rb_ref, x1_ref, h_ref, lg_ref = refs
    else:
        oa_ref, ob_ref, x_ref, mod_ref, ng_ref, w_ref, x1_ref, h_ref = refs
    y = jnp.dot(oa_ref[0], w_ref[0:sbw, :], preferred_element_type=F32)
    y = y + jnp.dot(ob_ref[0], w_ref[sbw:, :], preferred_element_type=F32)
    x1 = x_ref[0] + mod_ref[0, 0:1, :] * y
    x1_ref[0] = x1
    ms = jnp.mean(x1 * x1, axis=-1, keepdims=True)
    hn = x1 * lax.rsqrt(ms + EPS) * ng_ref[...]
    h = (hn * (1.0 + mod_ref[0, 2:3, :]) + mod_ref[0, 1:2, :]).astype(BF16)
    h_ref[0] = h
    if with_router:
        lg = lax.dot_general(rw_ref[...], h, (((1,), (1,)), ((), ())), preferred_element_type=F32)
        lg_ref[...] = lg[0:N_EXPERTS, :] + rb_ref[...]


def _outproj_call(oa, ob, x, mod, norm_g, w_out, router=None):
    b, s, d = x.shape
    sbw = oa.shape[-1]
    tm = min(TM_PROJ, s)
    row_spec = lambda w: pl.BlockSpec((1, tm, w), lambda i, j: (i, j, 0))
    const = lambda shape: pl.BlockSpec(shape, lambda i, j: (0,) * len(shape))
    in_specs = [row_spec(sbw), row_spec(ob.shape[-1]), row_spec(d),
                pl.BlockSpec((1, 3, d), lambda i, j: (i, 0, 0)), const((1, d)), const(w_out.shape)]
    out_specs = [row_spec(d), row_spec(d)]
    out_shape = [jax.ShapeDtypeStruct((b, s, d), F32), jax.ShapeDtypeStruct((b, s, d), BF16)]
    args = [oa, ob, x, mod, norm_g, w_out]
    if router is not None:
        rw_t, rb = router
        in_specs += [const(rw_t.shape), const(rb.shape)]
        nj = s // tm
        out_specs.append(pl.BlockSpec((N_EXPERTS, tm), lambda i, j: (0, i * nj + j)))
        out_shape.append(jax.ShapeDtypeStruct((N_EXPERTS, b * s), F32))
        args += [rw_t, rb]
    return pl.pallas_call(
        functools.partial(_outproj_kernel, sbw=sbw, with_router=router is not None),
        grid=(b, s // tm),
        in_specs=in_specs,
        out_specs=out_specs,
        out_shape=out_shape,
        compiler_params=_cparams(("parallel", "arbitrary")),
        name="mixer_out",
    )(*args)


def _swiglu_partial(h, wg_ref, wu_ref, wd_ref, parts):
    out = None
    for gcols, drows in parts:
        g = jnp.dot(h, wg_ref[gcols], preferred_element_type=F32)
        u = jnp.dot(h, wu_ref[gcols], preferred_element_type=F32)
        a = (g * jax.nn.sigmoid(g) * u).astype(BF16)
        y = jnp.dot(a, wd_ref[drows], preferred_element_type=F32)
        out = y if out is None else out + y
    return out


def _ffn_kernel(h_ref, x_ref, g2_ref, wg_ref, wu_ref, wd_ref, o_ref, acc_ref):
    h = h_ref[0]
    n_chunks = wg_ref.shape[0]
    for c0 in range(0, n_chunks, FFN_SUB):
        parts = [((c,), (c,)) for c in range(c0, min(c0 + FFN_SUB, n_chunks))]
        y = _swiglu_partial(h, wg_ref, wu_ref, wd_ref, parts)
        if c0 == 0:
            acc_ref[...] = y
        else:
            acc_ref[...] += y
    o_ref[0] = x_ref[0] + g2_ref[0] * acc_ref[...]


def _ffn_call(h, x, g2, wg, wu, wd):
    b, s, d = x.shape
    tm = min(TM_FFN, s)
    row_spec = pl.BlockSpec((1, tm, d), lambda i, j: (i, j, 0))
    resident = lambda a: pl.BlockSpec(a.shape, lambda i, j: (0, 0, 0), pipeline_mode=pl.Buffered(1))
    return pl.pallas_call(
        _ffn_kernel,
        grid=(b, s // tm),
        in_specs=[row_spec, row_spec, pl.BlockSpec((1, 1, d), lambda i, j: (i, 0, 0)),
                  resident(wg), resident(wu), resident(wd)],
        out_specs=row_spec,
        out_shape=jax.ShapeDtypeStruct((b, s, d), F32),
        scratch_shapes=[pltpu.VMEM((tm, d), F32)],
        compiler_params=_cparams(("parallel", "arbitrary")),
        name="dense_ffn",
    )(h, x, g2, wg, wu, wd)


def _route_kernel(lg_ref, oi_ref, of_ref, tile_ref, cnt_ref, carry_ref):
    step = pl.program_id(0)
    tl = lg_ref.shape[1]
    ne = N_EXPERTS

    @pl.when(step == 0)
    def _():
        carry_ref[...] = jnp.zeros_like(carry_ref)

    lg = lg_ref[...]
    eidx = lax.broadcasted_iota(jnp.int32, (ne, tl), 0).astype(F32)
    m1 = jnp.max(lg, axis=0, keepdims=True)
    i1 = jnp.min(jnp.where(lg == m1, eidx, float(ne)), axis=0, keepdims=True)
    lg2 = jnp.where(eidx == i1, -jnp.inf, lg)
    m2 = jnp.max(lg2, axis=0, keepdims=True)
    i2 = jnp.min(jnp.where(lg2 == m2, eidx, float(ne)), axis=0, keepdims=True)
    t = jnp.exp(m2 - m1)
    gate1 = 1.0 / (1.0 + t)
    gate2 = t / (1.0 + t)

    sel1 = eidx == i1
    sel2 = eidx == i2
    onehot = jnp.where(sel1 | sel2, 1.0, 0.0)
    a = lax.broadcasted_iota(jnp.int32, (tl, tl), 0)
    bcol = lax.broadcasted_iota(jnp.int32, (tl, tl), 1)
    upper = jnp.where(a < bcol, 1.0, 0.0).astype(BF16)
    lhs = jnp.concatenate([onehot, jnp.zeros_like(onehot)], axis=0).astype(BF16)
    excl_tile = jnp.dot(lhs, upper, preferred_element_type=F32)[0:ne, :]
    before = carry_ref[...]
    n_tile = jnp.sum(onehot, axis=1, keepdims=True) + jnp.zeros_like(before)
    n_tile = jnp.floor((n_tile + (SUBLANES - 1)) * (1.0 / SUBLANES)) * SUBLANES
    off = jnp.concatenate([jnp.zeros((1, LANES), F32)] +
                          [jnp.sum(n_tile[0:e], axis=0, keepdims=True) for e in range(1, ne)], axis=0)
    pick = lambda sel, v: jnp.sum(jnp.where(sel, v, 0.0), axis=0, keepdims=True)
    r1 = pick(sel1, excl_tile + before[:, 0:1])
    r2 = pick(sel2, excl_tile + before[:, 0:1])
    p1 = pick(sel1, excl_tile + off[:, 0:1])
    p2 = pick(sel2, excl_tile + off[:, 0:1])
    carry_ref[...] = before + n_tile

    as_i = lambda v: v.astype(jnp.int32)
    oi_ref[...] = jnp.concatenate(
        [as_i(i1), as_i(i2), as_i(r1), as_i(r2), as_i(p1), as_i(p2), jnp.zeros((ne - 6, tl), jnp.int32)],
        axis=0)
    of_ref[...] = jnp.concatenate([gate1, gate2, jnp.zeros((ne - 2, tl), F32)], axis=0)
    lane = lax.broadcasted_iota(jnp.int32, (ne, LANES), 1)
    tile_ref[0] = as_i(jnp.where(lane == 0, n_tile, jnp.where(lane == 1, before, 0.0)))
    cnt_ref[...] = carry_ref[...]


def _route_call(logits_t):
    ne, t = logits_t.shape
    tl = min(TT_ROWS, t)
    blk = pl.BlockSpec((ne, tl), lambda i: (0, i))
    return pl.pallas_call(
        _route_kernel,
        grid=(t // tl,),
        in_specs=[blk],
        out_specs=[blk, blk, pl.BlockSpec((1, ne, LANES), lambda i: (i, 0, 0)),
                   pl.BlockSpec((ne, LANES), lambda i: (0, 0))],
        out_shape=[jax.ShapeDtypeStruct((ne, t), jnp.int32), jax.ShapeDtypeStruct((ne, t), F32),
                   jax.ShapeDtypeStruct((t // tl, ne, LANES), jnp.int32),
                   jax.ShapeDtypeStruct((ne, LANES), F32)],
        scratch_shapes=[pltpu.VMEM((ne, LANES), F32)],
        compiler_params=_cparams(("arbitrary",)),
        name="moe_route",
    )(logits_t)


def _run_pieces(n, max_rows):
    size = max_rows
    while size >= SUBLANES:
        yield size, pl.multiple_of(n & ~(2 * size - 1), SUBLANES), (n & size) != 0
        size //= 2


def _compact_rows(tt):
    return 2 * tt + N_EXPERTS * SUBLANES


def _dispatch_kernel(tail_ref, cnt_ref, base_ref, pos_ref, h_ref, xb_ref, c_ref, zero_ref, sem, zsem):
    step = pl.program_id(0)
    tt = h_ref.shape[0]
    tmb = zero_ref.shape[0]

    def zero_block(blk):
        row = pl.multiple_of(blk * tmb, tmb)
        return pltpu.make_async_copy(zero_ref, xb_ref.at[pl.ds(row, tmb)], zsem)

    @pl.when(step == 0)
    def _():
        zero_ref[...] = jnp.zeros_like(zero_ref)
        for e in range(N_EXPERTS):
            zero_block(tail_ref[e]).start()
        for e in range(N_EXPERTS):
            zero_block(tail_ref[e]).wait()

        def zero_unused(blk, c):
            cp = zero_block(blk)
            cp.start()
            cp.wait()
            return c

        lax.fori_loop(tail_ref[N_EXPERTS], xb_ref.shape[0] // tmb, zero_unused, 0)

    r = lax.broadcasted_iota(jnp.int32, (c_ref.shape[0], tt), 0)
    onehot = jnp.where((r == pos_ref[4:5, :]) | (r == pos_ref[5:6, :]), 1.0, 0.0).astype(BF16)
    c_ref[...] = jnp.dot(onehot, h_ref[...], preferred_element_type=F32)

    def run_copies(op):
        off = 0
        for e in range(N_EXPERTS):
            n = cnt_ref[step * N_EXPERTS + e]
            dst = pl.multiple_of(base_ref[step * N_EXPERTS + e], SUBLANES)
            for size, piece, present in _run_pieces(n, tt):
                @pl.when(present)
                def _():
                    src = pl.multiple_of(off + piece, SUBLANES)
                    op(pltpu.make_async_copy(c_ref.at[pl.ds(src, size)],
                                             xb_ref.at[pl.ds(pl.multiple_of(dst + piece, SUBLANES), size)],
                                             sem))
            off = off + n

    run_copies(lambda cp: cp.start())
    run_copies(lambda cp: cp.wait())


def _dispatch_call(tail_blocks, tile_cnt, tile_base, pos, h, n_slots):
    t, d = h.shape
    tt = min(TT_ROWS, t)
    tmb = TM_MOE
    return pl.pallas_call(
        _dispatch_kernel,
        grid_spec=pltpu.PrefetchScalarGridSpec(
            num_scalar_prefetch=3,
            grid=(t // tt,),
            in_specs=[pl.BlockSpec((N_EXPERTS, tt), lambda i, *_: (0, i)),
                      pl.BlockSpec((tt, d), lambda i, *_: (i, 0))],
            out_specs=pl.BlockSpec(memory_space=pl.ANY),
            scratch_shapes=[pltpu.VMEM((_compact_rows(tt), d), F32), pltpu.VMEM((tmb, d), F32),
                            pltpu.SemaphoreType.DMA, pltpu.SemaphoreType.DMA],
        ),
        out_shape=jax.ShapeDtypeStruct((n_slots, d), F32),
        compiler_params=_cparams(("arbitrary",)),
        name="moe_dispatch",
    )(tail_blocks, tile_cnt, tile_base, pos, h)


def _moe_ffn_kernel(be_ref, nb_ref, x_ref, wg_ref, wu_ref, wd_ref, o_ref, xb_ref):
    i = pl.program_id(0)
    c = pl.program_id(1)
    used = i < nb_ref[0]
    tf = wg_ref.shape[2]
    wpart = min(tf, MXU_DIM)
    parts = [((0, slice(None), slice(p, p + wpart)), (0, slice(p, p + wpart), slice(None)))
             for p in range(0, tf, wpart)]

    @pl.when(used & (c == 0))
    def _():
        xb_ref[...] = x_ref[...].astype(BF16)
        o_ref[...] = _swiglu_partial(xb_ref[...], wg_ref, wu_ref, wd_ref, parts)

    @pl.when(used & (c != 0))
    def _():
        o_ref[...] += _swiglu_partial(xb_ref[...], wg_ref, wu_ref, wd_ref, parts)

    @pl.when(jnp.logical_not(used) & (c == 0))
    def _():
        o_ref[...] = jnp.zeros_like(o_ref)


def _moe_ffn_call(block_e, n_used, xb, wg, wu, wd):
    n_slots, d = xb.shape
    tmb = TM_MOE
    nb = n_slots // tmb
    f = wg.shape[2]
    tf = _largest_divisor(f, (512, 256, 128))
    nc = f // tf
    blk = lambda i, nbu: jnp.minimum(i, nbu[0] - 1)
    chunk = lambda i, c, nbu: jnp.where(i < nbu[0], c, nc - 1)
    return pl.pallas_call(
        _moe_ffn_kernel,
        grid_spec=pltpu.PrefetchScalarGridSpec(
            num_scalar_prefetch=2,
            grid=(nb, nc),
            in_specs=[
                pl.BlockSpec((tmb, d), lambda i, c, be, nbu: (blk(i, nbu), 0)),
                pl.BlockSpec((1, d, tf), lambda i, c, be, nbu: (be[blk(i, nbu)], 0, chunk(i, c, nbu))),
                pl.BlockSpec((1, d, tf), lambda i, c, be, nbu: (be[blk(i, nbu)], 0, chunk(i, c, nbu))),
                pl.BlockSpec((1, tf, d), lambda i, c, be, nbu: (be[blk(i, nbu)], chunk(i, c, nbu), 0)),
            ],
            out_specs=pl.BlockSpec((tmb, d), lambda i, c, be, nbu: (i, 0)),
            scratch_shapes=[pltpu.VMEM((tmb, d), BF16)],
        ),
        out_shape=jax.ShapeDtypeStruct((n_slots, d), F32),
        compiler_params=_cparams(("arbitrary", "arbitrary")),
        name="moe_ffn",
    )(block_e, n_used, xb, wg, wu, wd)


def _combine_kernel(dest_ref, yb_ref, x_ref, g2_ref, gate_ref, o_ref, rows_ref, sem):
    tt = x_ref.shape[1]

    def row_copy(t, k):
        return pltpu.make_async_copy(yb_ref.at[pl.ds(dest_ref[0, k, t], 1)],
                                     rows_ref.at[k, pl.ds(t, 1)], sem)

    def start(t, c):
        row_copy(t, 0).start()
        row_copy(t, 1).start()
        return c

    def wait(t, c):
        row_copy(t, 0).wait()
        row_copy(t, 1).wait()
        return c

    for t in range(tt):
        start(t, 0)
    lax.fori_loop(0, tt, wait, 0, unroll=8)
    y = rows_ref[0] * gate_ref[0, :, 0:1] + rows_ref[1] * gate_ref[0, :, 1:2]
    o_ref[0] = x_ref[0] + g2_ref[0] * y


def _combine_call(dest, yb, x, g2, gates):
    b, s, d = x.shape
    tt = dest.shape[2]
    nj = s // tt
    row_spec = pl.BlockSpec((1, tt, d), lambda i, j: (i, j, 0))
    return pl.pallas_call(
        _combine_kernel,
        grid=(b, nj),
        in_specs=[
            pl.BlockSpec((1, 2, tt), lambda i, j: (i * nj + j, 0, 0), memory_space=pltpu.SMEM),
            pl.BlockSpec(memory_space=pl.ANY),
            row_spec,
            pl.BlockSpec((1, 1, d), lambda i, j: (i, 0, 0)),
            pl.BlockSpec((1, tt, 2), lambda i, j: (i * nj + j, 0, 0)),
        ],
        out_specs=row_spec,
        out_shape=jax.ShapeDtypeStruct((b, s, d), F32),
        scratch_shapes=[pltpu.VMEM((2, tt, d), F32), pltpu.SemaphoreType.DMA],
        compiler_params=_cparams(("arbitrary", "arbitrary")),
        name="moe_combine",
    )(dest, yb, x, g2, gates)


def _chunk_cols(w, tf):
    d, f = w.shape
    return w.reshape(d, f // tf, tf).transpose(1, 0, 2).astype(BF16)


def _moe_layer(h, x1, g2, logits_t, wg, wu, wd):
    b, s, d = x1.shape
    t = b * s
    tmb = TM_MOE
    tt = min(TT_ROWS, s)
    ne = N_EXPERTS
    oi, of, tiles, cnt = _route_call(logits_t)
    e1, e2, r1, r2 = oi[0], oi[1], oi[2], oi[3]
    counts = cnt[:, 0].astype(jnp.int32)
    padded = (counts + tmb - 1) // tmb * tmb
    pad_end = jnp.cumsum(padded)
    pad_start = pad_end - padded
    n_used = (pad_end[-1] // tmb).astype(jnp.int32)
    nb = -(-(2 * t + (t // tt) * ne * (SUBLANES - 1)) // tmb) + ne
    block_row0 = jnp.arange(nb, dtype=jnp.int32) * tmb
    block_e = jnp.minimum(jnp.sum(pad_end[None, :] <= block_row0[:, None], axis=1), ne - 1).astype(jnp.int32)
    tail_blocks = jnp.where(padded > 0, pad_end // tmb - 1, nb - 1)
    tail_blocks = jnp.concatenate([tail_blocks, n_used[None]]).astype(jnp.int32)
    start_of = lambda e: jnp.sum(jnp.where(e[None, :] == jnp.arange(ne)[:, None], pad_start[:, None], 0), axis=0)
    d1 = start_of(e1) + r1
    d2 = start_of(e2) + r2
    dest = jnp.stack([d1.reshape(t // tt, tt), d2.reshape(t // tt, tt)], axis=1).astype(jnp.int32)

    tile_cnt = tiles[:, :, 0].reshape(-1)
    tile_base = (tiles[:, :, 1] + pad_start[None, :]).reshape(-1).astype(jnp.int32)
    xb = _dispatch_call(tail_blocks, tile_cnt, tile_base, oi, h.reshape(t, d), nb * tmb)
    yb = _moe_ffn_call(block_e, n_used.reshape(1), xb, wg, wu, wd)
    gates = jnp.stack([of[0], of[1]], axis=-1).reshape(t // tt, tt, 2)
    return _combine_call(dest, yb, x1, g2, gates)


def kernel(x, c, ada_w, ada_b, norm_mix_g, norm_ffn_g, w_in, q_norm_g, k_norm_g, sg_ln_g, sg_ln_b,
           sg_w_spatial, sg_b_spatial, w_out, ffn_w_gate, ffn_w_up, ffn_w_down, router_w, router_b,
           moe_w_gate, moe_w_up, moe_w_down):
    b, s, d = x.shape
    depth = ada_w.shape[0]
    sgw = sg_ln_g.shape[-1]
    sbw = w_out.shape[1] - sgw
    assert sbw % PAIR == 0 and sgw % PAIR == 0 and s % SEQ_BLOCK == 0
    assert w_in.shape[-1] == 3 * sbw + 2 * sgw

    ada = _ada_call(c, ada_w, ada_b)
    ada = ada.reshape(depth, b, 6, d)

    wb = MXU_DIM if sbw % MXU_DIM == 0 else PAIR
    gi = jnp.arange(wb) // HEAD_DIM
    bd = (gi[:, None] == gi[None, :]).astype(BF16)
    pos = jnp.arange(SEQ_BLOCK)
    chunk_causal = (pos[None, :] // CHUNK) <= (pos[:, None] // CHUNK)
    scale = HEAD_DIM ** -0.5

    for l in range(depth):
        sh1, sc1, g1, sh2, sc2, g2 = (ada[l, :, j] for j in range(6))
        qk_g = jnp.stack([jnp.tile(q_norm_g[l], sbw // HEAD_DIM) * scale,
                          jnp.tile(k_norm_g[l], sbw // HEAD_DIM)])
        ln = jnp.stack([sg_ln_g[l], sg_ln_b[l]])
        ws = jnp.where(chunk_causal[None], sg_w_spatial[l], 0.0)
        ws = ws.reshape(sgw // PAIR, 2, SEQ_BLOCK, SEQ_BLOCK).transpose(0, 2, 1, 3)
        ws = ws.reshape(sgw // PAIR, SEQ_BLOCK, 2 * SEQ_BLOCK).astype(BF16)
        bias = jnp.repeat(sg_b_spatial[l].T, HEAD_DIM, axis=1)

        q, k, v, ob = _inproj_call(x, jnp.stack([sh1, sc1], axis=1), norm_mix_g[l][None],
                                   w_in[l].astype(BF16), qk_g, bd, ln, ws, bias, sbw=sbw, sgw=sgw)
        oa = _attn_call(q, k, v)
        mod2 = jnp.stack([g1, sh2, sc2], axis=1)
        j = l // 2
        if l % 2 == 0:
            x1, h = _outproj_call(oa, ob, x, mod2, norm_ffn_g[l][None], w_out[l].astype(BF16))
            tf = _largest_divisor(ffn_w_gate.shape[-1], (512, 256, 128))
            wg = _chunk_cols(ffn_w_gate[j], tf)
            wu = _chunk_cols(ffn_w_up[j], tf)
            wd = ffn_w_down[j].reshape(-1, tf, d).astype(BF16)
            x = _ffn_call(h, x1, g2[:, None], wg, wu, wd)
        else:
            rw_t = jnp.zeros((2 * N_EXPERTS, d), BF16).at[:N_EXPERTS].set(router_w[j].T.astype(BF16))
            rb = router_b[j].reshape(N_EXPERTS, 1)
            x1, h, logits_t = _outproj_call(oa, ob, x, mod2, norm_ffn_g[l][None],
                                            w_out[l].astype(BF16), router=(rw_t, rb))
            x = _moe_layer(h, x1, g2[:, None], logits_t, moe_w_gate[j].astype(BF16),
                           moe_w_up[j].astype(BF16), moe_w_down[j].astype(BF16))
    return x
```

```python
import functools
import math

import jax
import jax.numpy as jnp
from jax import lax
from jax.experimental import pallas as pl
from jax.experimental.pallas import tpu as pltpu

F32 = jnp.float32
BF16 = jnp.bfloat16

HEAD_DIM = 64
PAIR = 2 * HEAD_DIM
SEQ_BLOCK = 128
CHUNK = 64
N_EXPERTS = 8
EPS = 1e-6
LOG2_E = 1.4426950408889634

LANES = 128
SUBLANES = 8
MXU_DIM = 256
VMEM_LIMIT_BYTES = 56 * 1024 * 1024

ATT_TILE = MXU_DIM
ATT_PAIRS = 4
ATT_KEY_TILES = 1
TM_PROJ = 1024
PROJ_PARTS = 1
TM_FFN = 512
TM_MOE = 1024
FFN_SUB = 2
TT_ROWS = 512


def _cparams(sem):
    return pltpu.CompilerParams(dimension_semantics=sem, vmem_limit_bytes=VMEM_LIMIT_BYTES)


def _split_bf16(x):
    hi = x.astype(BF16)
    lo = (x - hi.astype(F32)).astype(BF16)
    return hi, lo


def _largest_divisor(n, candidates):
    for c in candidates:
        if n % c == 0:
            return c
    raise ValueError(f"no tile in {candidates} divides {n}")


def _ada_kernel(c_ref, w_ref, b_ref, o_ref):
    c = c_ref[...]
    c_act = (c * jax.nn.sigmoid(c)).astype(BF16)
    o_ref[0] = jnp.dot(c_act, w_ref[0].astype(BF16), preferred_element_type=F32) + b_ref[0]


def _ada_call(c, ada_w, ada_b):
    depth, d, n = ada_w.shape
    b = c.shape[0]
    tn = _largest_divisor(n, (1536, 1024, 512, 256, 128))
    return pl.pallas_call(
        _ada_kernel,
        grid=(depth, n // tn),
        in_specs=[
            pl.BlockSpec((b, d), lambda l, j: (0, 0)),
            pl.BlockSpec((1, d, tn), lambda l, j: (l, 0, j)),
            pl.BlockSpec((1, 1, tn), lambda l, j: (l, 0, j)),
        ],
        out_specs=pl.BlockSpec((1, b, tn), lambda l, j: (l, 0, j)),
        out_shape=jax.ShapeDtypeStruct((depth, b, n), F32),
        compiler_params=_cparams(("arbitrary", "arbitrary")),
        name="ada_ln",
    )(c, ada_w, ada_b.reshape(depth, 1, n))


def _group_mean_sq(t, bd):
    wb = bd.shape[0]
    sq = t * t
    hi, lo = _split_bf16(sq)
    cols = []
    for c0 in range(0, t.shape[1], wb):
        s = jnp.dot(hi[:, c0:c0 + wb], bd, preferred_element_type=F32)
        s = s + jnp.dot(lo[:, c0:c0 + wb], bd, preferred_element_type=F32)
        cols.append(s)
    ss = cols[0] if len(cols) == 1 else jnp.concatenate(cols, axis=1)
    return ss * (1.0 / HEAD_DIM)


def _gelu(t):
    return 0.5 * t * (1.0 + lax.erf(t * math.sqrt(0.5)))


def _inproj_kernel(x_ref, mod_ref, ng_ref, w_ref, qkg_ref, bd_ref, ln_ref, ws_ref, bias_ref,
                   q_ref, k_ref, v_ref, ob_ref, *, sbw, sgw):
    tm = x_ref.shape[1]
    for part in _row_parts(tm):
        _inproj_rows(part, x_ref, mod_ref, ng_ref, w_ref, qkg_ref, bd_ref, ln_ref, ws_ref, bias_ref,
                     q_ref, k_ref, v_ref, ob_ref, sbw, sgw)


def _row_parts(tm):
    rows = max(tm // PROJ_PARTS, SEQ_BLOCK)
    return [slice(r, r + rows) for r in range(0, tm, rows)]


def _inproj_rows(part, x_ref, mod_ref, ng_ref, w_ref, qkg_ref, bd_ref, ln_ref, ws_ref, bias_ref,
                 q_ref, k_ref, v_ref, ob_ref, sbw, sgw):
    xf = x_ref[0, part, :]
    ms = jnp.mean(xf * xf, axis=-1, keepdims=True)
    y = xf * lax.rsqrt(ms + EPS) * ng_ref[...]
    h = (y * (1.0 + mod_ref[0, 1:2, :]) + mod_ref[0, 0:1, :]).astype(BF16)

    bd = bd_ref[...]
    q = jnp.dot(h, w_ref[:, 0:sbw], preferred_element_type=F32)
    qn = q * lax.rsqrt(_group_mean_sq(q, bd) + EPS) * qkg_ref[0:1, :]
    q_ref[0, part, :] = qn.astype(BF16)
    k = jnp.dot(h, w_ref[:, sbw:2 * sbw], preferred_element_type=F32)
    kn = k * lax.rsqrt(_group_mean_sq(k, bd) + EPS) * qkg_ref[1:2, :]
    k_ref[0, part, :] = kn.astype(BF16)
    v_ref[0, part, :] = jnp.dot(h, w_ref[:, 2 * sbw:3 * sbw], preferred_element_type=F32).astype(BF16)

    u = _gelu(jnp.dot(h, w_ref[:, 3 * sbw:3 * sbw + sgw], preferred_element_type=F32))
    g = _gelu(jnp.dot(h, w_ref[:, 3 * sbw + sgw:3 * sbw + 2 * sgw], preferred_element_type=F32))
    mu = jnp.mean(g, axis=-1, keepdims=True)
    gc = g - mu
    var = jnp.mean(gc * gc, axis=-1, keepdims=True)
    gn = (gc * lax.rsqrt(var + EPS) * ln_ref[0:1, :] + ln_ref[1:2, :]).astype(BF16)

    first = lax.broadcasted_iota(jnp.int32, (SEQ_BLOCK, PAIR), 1) < HEAD_DIM
    zero = jnp.zeros((SEQ_BLOCK, PAIR), BF16)
    for r in range(xf.shape[0] // SEQ_BLOCK):
        rows = slice(r * SEQ_BLOCK, (r + 1) * SEQ_BLOCK)
        out_rows = slice(part.start + r * SEQ_BLOCK, part.start + (r + 1) * SEQ_BLOCK)
        for p in range(sgw // PAIR):
            cols = slice(p * PAIR, (p + 1) * PAIR)
            vb = gn[rows, cols]
            rhs = jnp.concatenate([jnp.where(first, vb, zero), jnp.where(first, zero, vb)], axis=0)
            mixed = jnp.dot(ws_ref[p], rhs, preferred_element_type=F32)
            ob_ref[0, out_rows, cols] = (u[rows, cols] * (mixed + bias_ref[:, cols])).astype(BF16)


def _inproj_call(x, mod, norm_g, w_in, qk_g, bd, ln, ws, bias, *, sbw, sgw):
    b, s, d = x.shape
    tm = min(TM_PROJ, s)
    n_in = w_in.shape[1]
    out = jax.ShapeDtypeStruct((b, s, sbw), BF16)
    row_spec = lambda w: pl.BlockSpec((1, tm, w), lambda i, j: (i, j, 0))
    const = lambda shape: pl.BlockSpec(shape, lambda i, j: (0,) * len(shape))
    return pl.pallas_call(
        functools.partial(_inproj_kernel, sbw=sbw, sgw=sgw),
        grid=(b, s // tm),
        in_specs=[
            row_spec(d),
            pl.BlockSpec((1, 2, d), lambda i, j: (i, 0, 0)),
            const((1, d)),
            const((d, n_in)),
            const((2, sbw)),
            const(bd.shape),
            const((2, sgw)),
            const(ws.shape),
            const((SEQ_BLOCK, sgw)),
        ],
        out_specs=[row_spec(sbw), row_spec(sbw), row_spec(sbw), row_spec(sgw)],
        out_shape=[out, out, out, jax.ShapeDtypeStruct((b, s, sgw), BF16)],
        compiler_params=_cparams(("parallel", "arbitrary")),
        name="mixer_in",
    )(x, mod, norm_g, w_in, qk_g, bd, ln, ws, bias)


def _attn_kernel(q_ref, k_ref, v_ref, o_ref, acc_ref, carry_ref):
    i = pl.program_id(2)
    tq = ATT_TILE
    hb = SEQ_BLOCK
    n_sub = tq // hb
    n_pairs = q_ref.shape[2] // PAIR
    first = lax.broadcasted_iota(jnp.int32, (tq, PAIR), 1) < HEAD_DIM
    zero = jnp.zeros((tq, PAIR), BF16)
    q2 = []
    for p in range(n_pairs):
        q = q_ref[0, :, p * PAIR:(p + 1) * PAIR]
        q2.append(jnp.concatenate([jnp.where(first, q, zero), jnp.where(first, zero, q)], axis=0))

    kk = lax.broadcasted_iota(jnp.int32, (2 * hb, 2 * hb), 0) & (hb - 1)
    nn = lax.broadcasted_iota(jnp.int32, (2 * hb, 2 * hb), 1)
    csum = jnp.where((kk >= nn) | (nn >= hb), -1.0, 0.0).astype(BF16)

    qpos = lax.broadcasted_iota(jnp.int32, (2 * tq, tq), 0) & (tq - 1)
    kpos = lax.broadcasted_iota(jnp.int32, (2 * tq, tq), 1)
    strict = kpos < qpos

    def weights(j, p, carry, diagonal):
        start = pl.multiple_of(j * tq, tq)
        kb = k_ref[0, pl.ds(start, tq), p * PAIR:(p + 1) * PAIR]
        z = lax.dot_general(q2[p], kb, (((1,), (1,)), ((), ())), preferred_element_type=F32)
        sp = jnp.maximum(z, 0.0) + jnp.log(1.0 + jnp.exp2(jnp.abs(z) * (-LOG2_E)))
        if diagonal:
            sp = jnp.where(strict, sp, 0.0)
        hi, lo = _split_bf16(sp)
        la = [None] * n_sub
        for sb in reversed(range(n_sub)):
            cols = slice(sb * hb, (sb + 1) * hb)
            lhs = jnp.concatenate([hi[:, cols], lo[:, cols]], axis=1)
            cs = jnp.dot(lhs, csum, preferred_element_type=F32)
            la[sb] = cs[:, :hb] if carry is None else cs[:, :hb] + carry
            carry = cs[:, hb:] if carry is None else carry + cs[:, hb:]
        w = jnp.exp(z + jnp.concatenate(la, axis=1))
        if diagonal:
            w = jnp.where(strict, w, 0.0)
        return w.astype(BF16), carry

    def sweep(j_hi, n_tiles, diagonal):
        start = pl.multiple_of((j_hi - (n_tiles - 1)) * tq, tq)
        for p in range(n_pairs):
            carry = None if diagonal else carry_ref[p]
            ws = []
            for n in range(n_tiles):
                w, carry = weights(j_hi - n, p, carry, diagonal and n == 0)
                ws.insert(0, w)
            carry_ref[p] = carry
            wcat = ws[0] if n_tiles == 1 else jnp.concatenate(ws, axis=1)
            vb = v_ref[0, pl.ds(start, n_tiles * tq), p * PAIR:(p + 1) * PAIR]
            pv = jnp.dot(wcat, vb, preferred_element_type=F32)
            if diagonal:
                acc_ref[p] = pv
            else:
                acc_ref[p] += pv

    sweep(i, 1, True)

    def body(t, c):
        sweep(i - 1 - ATT_KEY_TILES * t, ATT_KEY_TILES, False)
        return c

    lax.fori_loop(0, i // ATT_KEY_TILES, body, 0)
    for rem in range(1, ATT_KEY_TILES):
        @pl.when(i % ATT_KEY_TILES == rem)
        def _():
            sweep(rem - 1, rem, False)
    for p in range(n_pairs):
        o_ref[0, :, p * PAIR:(p + 1) * PAIR] = jnp.where(
            first, acc_ref[p, 0:tq, :], acc_ref[p, tq:2 * tq, :]).astype(BF16)


def _attn_call(q, k, v):
    b, s, w = q.shape
    tq = ATT_TILE
    wl = ATT_PAIRS * PAIR if w % (ATT_PAIRS * PAIR) == 0 else PAIR
    assert s % tq == 0
    return pl.pallas_call(
        _attn_kernel,
        grid=(b, w // wl, s // tq),
        in_specs=[
            pl.BlockSpec((1, tq, wl), lambda bi, p, i: (bi, i, p)),
            pl.BlockSpec((1, s, wl), lambda bi, p, i: (bi, 0, p)),
            pl.BlockSpec((1, s, wl), lambda bi, p, i: (bi, 0, p)),
        ],
        out_specs=pl.BlockSpec((1, tq, wl), lambda bi, p, i: (bi, i, p)),
        out_shape=jax.ShapeDtypeStruct((b, s, w), BF16),
        scratch_shapes=[pltpu.VMEM((wl // PAIR, 2 * tq, PAIR), F32),
                        pltpu.VMEM((wl // PAIR, 2 * tq, SEQ_BLOCK), F32)],
        compiler_params=_cparams(("parallel", "parallel", "arbitrary")),
        name="sb_attn",
    )(q, k, v)


def _outproj_kernel(*refs, sbw, with_router):
    if with_router:
        oa_ref, ob_ref, x_ref, mod_ref, ng_ref, w_ref, rw_ref, rb_ref, x1_ref, h_ref, lg_ref = refs
    else:
        oa_ref, ob_ref, x_ref, mod_ref, ng_ref, w_ref, x1_ref, h_ref = refs
    for part in _row_parts(x_ref.shape[1]):
        y = jnp.dot(oa_ref[0, part, :], w_ref[0:sbw, :], preferred_element_type=F32)
        y = y + jnp.dot(ob_ref[0, part, :], w_ref[sbw:, :], preferred_element_type=F32)
        x1 = x_ref[0, part, :] + mod_ref[0, 0:1, :] * y
        x1_ref[0, part, :] = x1
        ms = jnp.mean(x1 * x1, axis=-1, keepdims=True)
        hn = x1 * lax.rsqrt(ms + EPS) * ng_ref[...]
        h = (hn * (1.0 + mod_ref[0, 2:3, :]) + mod_ref[0, 1:2, :]).astype(BF16)
        h_ref[0, part, :] = h
        if with_router:
            lg = lax.dot_general(rw_ref[...], h, (((1,), (1,)), ((), ())), preferred_element_type=F32)
            lg_ref[:, part] = lg[0:N_EXPERTS, :] + rb_ref[...]


def _outproj_call(oa, ob, x, mod, norm_g, w_out, router=None):
    b, s, d = x.shape
    sbw = oa.shape[-1]
    tm = min(TM_PROJ, s)
    row_spec = lambda w: pl.BlockSpec((1, tm, w), lambda i, j: (i, j, 0))
    const = lambda shape: pl.BlockSpec(shape, lambda i, j: (0,) * len(shape))
    in_specs = [row_spec(sbw), row_spec(ob.shape[-1]), row_spec(d),
                pl.BlockSpec((1, 3, d), lambda i, j: (i, 0, 0)), const((1, d)), const(w_out.shape)]
    out_specs = [row_spec(d), row_spec(d)]
    out_shape = [jax.ShapeDtypeStruct((b, s, d), F32), jax.ShapeDtypeStruct((b, s, d), BF16)]
    args = [oa, ob, x, mod, norm_g, w_out]
    if router is not None:
        rw_t, rb = router
        in_specs += [const(rw_t.shape), const(rb.shape)]
        nj = s // tm
        out_specs.append(pl.BlockSpec((N_EXPERTS, tm), lambda i, j: (0, i * nj + j)))
        out_shape.append(jax.ShapeDtypeStruct((N_EXPERTS, b * s), F32))
        args += [rw_t, rb]
    return pl.pallas_call(
        functools.partial(_outproj_kernel, sbw=sbw, with_router=router is not None),
        grid=(b, s // tm),
        in_specs=in_specs,
        out_specs=out_specs,
        out_shape=out_shape,
        compiler_params=_cparams(("parallel", "arbitrary")),
        name="mixer_out",
    )(*args)


def _swiglu_partial(h, wg_ref, wu_ref, wd_ref, parts):
    out = None
    for gcols, drows in parts:
        g = jnp.dot(h, wg_ref[gcols], preferred_element_type=F32)
        u = jnp.dot(h, wu_ref[gcols], preferred_element_type=F32)
        a = (g * jax.nn.sigmoid(g) * u).astype(BF16)
        y = jnp.dot(a, wd_ref[drows], preferred_element_type=F32)
        out = y if out is None else out + y
    return out


def _ffn_kernel(h_ref, x_ref, g2_ref, wg_ref, wu_ref, wd_ref, o_ref, acc_ref):
    h = h_ref[0]
    n_chunks = wg_ref.shape[0]
    for c0 in range(0, n_chunks, FFN_SUB):
        parts = [((c,), (c,)) for c in range(c0, min(c0 + FFN_SUB, n_chunks))]
        y = _swiglu_partial(h, wg_ref, wu_ref, wd_ref, parts)
        if c0 == 0:
            acc_ref[...] = y
        else:
            acc_ref[...] += y
    o_ref[0] = x_ref[0] + g2_ref[0] * acc_ref[...]


def _ffn_call(h, x, g2, wg, wu, wd):
    b, s, d = x.shape
    tm = min(TM_FFN, s)
    row_spec = pl.BlockSpec((1, tm, d), lambda i, j: (i, j, 0))
    resident = lambda a: pl.BlockSpec(a.shape, lambda i, j: (0, 0, 0), pipeline_mode=pl.Buffered(1))
    return pl.pallas_call(
        _ffn_kernel,
        grid=(b, s // tm),
        in_specs=[row_spec, row_spec, pl.BlockSpec((1, 1, d), lambda i, j: (i, 0, 0)),
                  resident(wg), resident(wu), resident(wd)],
        out_specs=row_spec,
        out_shape=jax.ShapeDtypeStruct((b, s, d), F32),
        scratch_shapes=[pltpu.VMEM((tm, d), F32)],
        compiler_params=_cparams(("parallel", "arbitrary")),
        name="dense_ffn",
    )(h, x, g2, wg, wu, wd)


def _route_kernel(lg_ref, oi_ref, of_ref, tile_ref, cnt_ref, carry_ref):
    step = pl.program_id(0)
    tl = lg_ref.shape[1]
    ne = N_EXPERTS

    @pl.when(step == 0)
    def _():
        carry_ref[...] = jnp.zeros_like(carry_ref)

    lg = lg_ref[...]
    eidx = lax.broadcasted_iota(jnp.int32, (ne, tl), 0).astype(F32)
    m1 = jnp.max(lg, axis=0, keepdims=True)
    i1 = jnp.min(jnp.where(lg == m1, eidx, float(ne)), axis=0, keepdims=True)
    lg2 = jnp.where(eidx == i1, -jnp.inf, lg)
    m2 = jnp.max(lg2, axis=0, keepdims=True)
    i2 = jnp.min(jnp.where(lg2 == m2, eidx, float(ne)), axis=0, keepdims=True)
    t = jnp.exp(m2 - m1)
    gate1 = 1.0 / (1.0 + t)
    gate2 = t / (1.0 + t)

    sel1 = eidx == i1
    sel2 = eidx == i2
    onehot = jnp.where(sel1 | sel2, 1.0, 0.0)
    a = lax.broadcasted_iota(jnp.int32, (tl, tl), 0)
    bcol = lax.broadcasted_iota(jnp.int32, (tl, tl), 1)
    upper = jnp.where(a < bcol, 1.0, 0.0).astype(BF16)
    lhs = jnp.concatenate([onehot, jnp.zeros_like(onehot)], axis=0).astype(BF16)
    excl_tile = jnp.dot(lhs, upper, preferred_element_type=F32)[0:ne, :]
    before = carry_ref[...]
    n_tile = jnp.sum(onehot, axis=1, keepdims=True) + jnp.zeros_like(before)
    n_tile = jnp.floor((n_tile + (SUBLANES - 1)) * (1.0 / SUBLANES)) * SUBLANES
    off = jnp.concatenate([jnp.zeros((1, LANES), F32)] +
                          [jnp.sum(n_tile[0:e], axis=0, keepdims=True) for e in range(1, ne)], axis=0)
    pick = lambda sel, v: jnp.sum(jnp.where(sel, v, 0.0), axis=0, keepdims=True)
    r1 = pick(sel1, excl_tile + before[:, 0:1])
    r2 = pick(sel2, excl_tile + before[:, 0:1])
    p1 = pick(sel1, excl_tile + off[:, 0:1])
    p2 = pick(sel2, excl_tile + off[:, 0:1])
    carry_ref[...] = before + n_tile

    as_i = lambda v: v.astype(jnp.int32)
    oi_ref[...] = jnp.concatenate(
        [as_i(i1), as_i(i2), as_i(r1), as_i(r2), as_i(p1), as_i(p2), jnp.zeros((ne - 6, tl), jnp.int32)],
        axis=0)
    of_ref[...] = jnp.concatenate([gate1, gate2, jnp.zeros((ne - 2, tl), F32)], axis=0)
    lane = lax.broadcasted_iota(jnp.int32, (ne, LANES), 1)
    tile_ref[0] = as_i(jnp.where(lane == 0, n_tile, jnp.where(lane == 1, before, 0.0)))
    cnt_ref[...] = carry_ref[...]


def _route_call(logits_t):
    ne, t = logits_t.shape
    tl = min(TT_ROWS, t)
    blk = pl.BlockSpec((ne, tl), lambda i: (0, i))
    return pl.pallas_call(
        _route_kernel,
        grid=(t // tl,),
        in_specs=[blk],
        out_specs=[blk, blk, pl.BlockSpec((1, ne, LANES), lambda i: (i, 0, 0)),
                   pl.BlockSpec((ne, LANES), lambda i: (0, 0))],
        out_shape=[jax.ShapeDtypeStruct((ne, t), jnp.int32), jax.ShapeDtypeStruct((ne, t), F32),
                   jax.ShapeDtypeStruct((t // tl, ne, LANES), jnp.int32),
                   jax.ShapeDtypeStruct((ne, LANES), F32)],
        scratch_shapes=[pltpu.VMEM((ne, LANES), F32)],
        compiler_params=_cparams(("arbitrary",)),
        name="moe_route",
    )(logits_t)


def _run_pieces(n, max_rows):
    size = max_rows
    while size >= SUBLANES:
        yield size, pl.multiple_of(n & ~(2 * size - 1), SUBLANES), (n & size) != 0
        size //= 2


def _compact_rows(tt):
    return 2 * tt + N_EXPERTS * SUBLANES


def _dispatch_kernel(tail_ref, cnt_ref, base_ref, pos_ref, h_ref, xb_ref, c_ref, zero_ref, sem, zsem):
    step = pl.program_id(0)
    tt = h_ref.shape[0]
    tmb = zero_ref.shape[0]

    def zero_block(blk):
        row = pl.multiple_of(blk * tmb, tmb)
        return pltpu.make_async_copy(zero_ref, xb_ref.at[pl.ds(row, tmb)], zsem)

    @pl.when(step == 0)
    def _():
        zero_ref[...] = jnp.zeros_like(zero_ref)
        for e in range(N_EXPERTS):
            zero_block(tail_ref[e]).start()
        for e in range(N_EXPERTS):
            zero_block(tail_ref[e]).wait()

        def zero_unused(blk, c):
            cp = zero_block(blk)
            cp.start()
            cp.wait()
            return c

        lax.fori_loop(tail_ref[N_EXPERTS], xb_ref.shape[0] // tmb, zero_unused, 0)

    slot = step % 2
    r = lax.broadcasted_iota(jnp.int32, (c_ref.shape[1], tt), 0)
    onehot = jnp.where((r == pos_ref[4:5, :]) | (r == pos_ref[5:6, :]), 1.0, 0.0).astype(BF16)
    c_ref[slot] = jnp.dot(onehot, h_ref[...], preferred_element_type=F32)

    def run_copies(op, tile, buf):
        off = 0
        for e in range(N_EXPERTS):
            n = cnt_ref[tile * N_EXPERTS + e]
            dst = pl.multiple_of(base_ref[tile * N_EXPERTS + e], SUBLANES)
            for size, piece, present in _run_pieces(n, tt):
                @pl.when(present)
                def _():
                    src = pl.multiple_of(off + piece, SUBLANES)
                    op(pltpu.make_async_copy(c_ref.at[buf, pl.ds(src, size)],
                                             xb_ref.at[pl.ds(pl.multiple_of(dst + piece, SUBLANES), size)],
                                             sem.at[buf]))
            off = off + n

    run_copies(lambda cp: cp.start(), step, slot)

    @pl.when(step > 0)
    def _():
        run_copies(lambda cp: cp.wait(), step - 1, 1 - slot)

    @pl.when(step == pl.num_programs(0) - 1)
    def _():
        run_copies(lambda cp: cp.wait(), step, slot)


def _dispatch_call(tail_blocks, tile_cnt, tile_base, pos, h, n_slots):
    t, d = h.shape
    tt = min(TT_ROWS, t)
    tmb = TM_MOE
    return pl.pallas_call(
        _dispatch_kernel,
        grid_spec=pltpu.PrefetchScalarGridSpec(
            num_scalar_prefetch=3,
            grid=(t // tt,),
            in_specs=[pl.BlockSpec((N_EXPERTS, tt), lambda i, *_: (0, i)),
                      pl.BlockSpec((tt, d), lambda i, *_: (i, 0))],
            out_specs=pl.BlockSpec(memory_space=pl.ANY),
            scratch_shapes=[pltpu.VMEM((2, _compact_rows(tt), d), F32), pltpu.VMEM((tmb, d), F32),
                            pltpu.SemaphoreType.DMA((2,)), pltpu.SemaphoreType.DMA],
        ),
        out_shape=jax.ShapeDtypeStruct((n_slots, d), F32),
        compiler_params=_cparams(("arbitrary",)),
        name="moe_dispatch",
    )(tail_blocks, tile_cnt, tile_base, pos, h)


def _moe_ffn_kernel(be_ref, nb_ref, x_ref, wg_ref, wu_ref, wd_ref, o_ref, xb_ref):
    i = pl.program_id(0)
    c = pl.program_id(1)
    used = i < nb_ref[0]
    tf = wg_ref.shape[2]
    wpart = min(tf, MXU_DIM)
    parts = [((0, slice(None), slice(p, p + wpart)), (0, slice(p, p + wpart), slice(None)))
             for p in range(0, tf, wpart)]

    @pl.when(used & (c == 0))
    def _():
        xb_ref[...] = x_ref[...].astype(BF16)
        o_ref[...] = _swiglu_partial(xb_ref[...], wg_ref, wu_ref, wd_ref, parts)

    @pl.when(used & (c != 0))
    def _():
        o_ref[...] += _swiglu_partial(xb_ref[...], wg_ref, wu_ref, wd_ref, parts)

    @pl.when(jnp.logical_not(used) & (c == 0))
    def _():
        o_ref[...] = jnp.zeros_like(o_ref)


def _moe_ffn_call(block_e, n_used, xb, wg, wu, wd):
    n_slots, d = xb.shape
    tmb = TM_MOE
    nb = n_slots // tmb
    f = wg.shape[2]
    tf = _largest_divisor(f, (512, 256, 128))
    nc = f // tf
    blk = lambda i, nbu: jnp.minimum(i, nbu[0] - 1)
    chunk = lambda i, c, nbu: jnp.where(i < nbu[0], c, nc - 1)
    return pl.pallas_call(
        _moe_ffn_kernel,
        grid_spec=pltpu.PrefetchScalarGridSpec(
            num_scalar_prefetch=2,
            grid=(nb, nc),
            in_specs=[
                pl.BlockSpec((tmb, d), lambda i, c, be, nbu: (blk(i, nbu), 0)),
                pl.BlockSpec((1, d, tf), lambda i, c, be, nbu: (be[blk(i, nbu)], 0, chunk(i, c, nbu))),
                pl.BlockSpec((1, d, tf), lambda i, c, be, nbu: (be[blk(i, nbu)], 0, chunk(i, c, nbu))),
                pl.BlockSpec((1, tf, d), lambda i, c, be, nbu: (be[blk(i, nbu)], chunk(i, c, nbu), 0)),
            ],
            out_specs=pl.BlockSpec((tmb, d), lambda i, c, be, nbu: (i, 0)),
            scratch_shapes=[pltpu.VMEM((tmb, d), BF16)],
        ),
        out_shape=jax.ShapeDtypeStruct((n_slots, d), F32),
        compiler_params=_cparams(("arbitrary", "arbitrary")),
        name="moe_ffn",
    )(block_e, n_used, xb, wg, wu, wd)


def _combine_kernel(dest_ref, yb_ref, x_ref, g2_ref, gate_ref, o_ref, rows_ref, sem):
    tt = x_ref.shape[1]

    def row_copy(t, k):
        return pltpu.make_async_copy(yb_ref.at[pl.ds(dest_ref[0, k, t], 1)],
                                     rows_ref.at[k, pl.ds(t, 1)], sem)

    def wait(t, c):
        row_copy(t, 0).wait()
        row_copy(t, 1).wait()
        return c

    for t in range(tt):
        row_copy(t, 0).start(priority=0)
        row_copy(t, 1).start(priority=1)
    lax.fori_loop(0, tt, wait, 0, unroll=8)
    y = rows_ref[0] * gate_ref[0, :, 0:1] + rows_ref[1] * gate_ref[0, :, 1:2]
    o_ref[0] = x_ref[0] + g2_ref[0] * y


def _combine_call(dest, yb, x, g2, gates):
    b, s, d = x.shape
    tt = dest.shape[2]
    nj = s // tt
    row_spec = pl.BlockSpec((1, tt, d), lambda i, j: (i, j, 0))
    return pl.pallas_call(
        _combine_kernel,
        grid=(b, nj),
        in_specs=[
            pl.BlockSpec((1, 2, tt), lambda i, j: (i * nj + j, 0, 0), memory_space=pltpu.SMEM),
            pl.BlockSpec(memory_space=pl.ANY),
            row_spec,
            pl.BlockSpec((1, 1, d), lambda i, j: (i, 0, 0)),
            pl.BlockSpec((1, tt, 2), lambda i, j: (i * nj + j, 0, 0)),
        ],
        out_specs=row_spec,
        out_shape=jax.ShapeDtypeStruct((b, s, d), F32),
        scratch_shapes=[pltpu.VMEM((2, tt, d), F32), pltpu.SemaphoreType.DMA],
        compiler_params=_cparams(("arbitrary", "arbitrary")),
        name="moe_combine",
    )(dest, yb, x, g2, gates)


def _chunk_cols(w, tf):
    d, f = w.shape
    return w.reshape(d, f // tf, tf).transpose(1, 0, 2).astype(BF16)


def _moe_layer(h, x1, g2, logits_t, wg, wu, wd):
    b, s, d = x1.shape
    t = b * s
    tmb = TM_MOE
    tt = min(TT_ROWS, s)
    ne = N_EXPERTS
    oi, of, tiles, cnt = _route_call(logits_t)
    e1, e2, r1, r2 = oi[0], oi[1], oi[2], oi[3]
    counts = cnt[:, 0].astype(jnp.int32)
    padded = (counts + tmb - 1) // tmb * tmb
    pad_end = jnp.cumsum(padded)
    pad_start = pad_end - padded
    n_used = (pad_end[-1] // tmb).astype(jnp.int32)
    nb = -(-(2 * t + (t // tt) * ne * (SUBLANES - 1)) // tmb) + ne
    block_row0 = jnp.arange(nb, dtype=jnp.int32) * tmb
    block_e = jnp.minimum(jnp.sum(pad_end[None, :] <= block_row0[:, None], axis=1), ne - 1).astype(jnp.int32)
    tail_blocks = jnp.where(padded > 0, pad_end // tmb - 1, nb - 1)
    tail_blocks = jnp.concatenate([tail_blocks, n_used[None]]).astype(jnp.int32)
    start_of = lambda e: jnp.sum(jnp.where(e[None, :] == jnp.arange(ne)[:, None], pad_start[:, None], 0), axis=0)
    d1 = start_of(e1) + r1
    d2 = start_of(e2) + r2
    dest = jnp.stack([d1.reshape(t // tt, tt), d2.reshape(t // tt, tt)], axis=1).astype(jnp.int32)

    tile_cnt = tiles[:, :, 0].reshape(-1)
    tile_base = (tiles[:, :, 1] + pad_start[None, :]).reshape(-1).astype(jnp.int32)
    xb = _dispatch_call(tail_blocks, tile_cnt, tile_base, oi, h.reshape(t, d), nb * tmb)
    yb = _moe_ffn_call(block_e, n_used.reshape(1), xb, wg, wu, wd)
    gates = jnp.stack([of[0], of[1]], axis=-1).reshape(t // tt, tt, 2)
    return _combine_call(dest, yb, x1, g2, gates)


def kernel(x, c, ada_w, ada_b, norm_mix_g, norm_ffn_g, w_in, q_norm_g, k_norm_g, sg_ln_g, sg_ln_b,
           sg_w_spatial, sg_b_spatial, w_out, ffn_w_gate, ffn_w_up, ffn_w_down, router_w, router_b,
           moe_w_gate, moe_w_up, moe_w_down):
    b, s, d = x.shape
    depth = ada_w.shape[0]
    sgw = sg_ln_g.shape[-1]
    sbw = w_out.shape[1] - sgw
    assert sbw % PAIR == 0 and sgw % PAIR == 0 and s % SEQ_BLOCK == 0
    assert w_in.shape[-1] == 3 * sbw + 2 * sgw

    ada = _ada_call(c, ada_w, ada_b)
    ada = ada.reshape(depth, b, 6, d)

    wb = MXU_DIM if sbw % MXU_DIM == 0 else PAIR
    gi = jnp.arange(wb) // HEAD_DIM
    bd = (gi[:, None] == gi[None, :]).astype(BF16)
    pos = jnp.arange(SEQ_BLOCK)
    chunk_causal = (pos[None, :] // CHUNK) <= (pos[:, None] // CHUNK)
    scale = HEAD_DIM ** -0.5

    for l in range(depth):
        sh1, sc1, g1, sh2, sc2, g2 = (ada[l, :, j] for j in range(6))
        qk_g = jnp.stack([jnp.tile(q_norm_g[l], sbw // HEAD_DIM) * scale,
                          jnp.tile(k_norm_g[l], sbw // HEAD_DIM)])
        ln = jnp.stack([sg_ln_g[l], sg_ln_b[l]])
        ws = jnp.where(chunk_causal[None], sg_w_spatial[l], 0.0)
        ws = ws.reshape(sgw // PAIR, 2, SEQ_BLOCK, SEQ_BLOCK).transpose(0, 2, 1, 3)
        ws = ws.reshape(sgw // PAIR, SEQ_BLOCK, 2 * SEQ_BLOCK).astype(BF16)
        bias = jnp.repeat(sg_b_spatial[l].T, HEAD_DIM, axis=1)

        q, k, v, ob = _inproj_call(x, jnp.stack([sh1, sc1], axis=1), norm_mix_g[l][None],
                                   w_in[l].astype(BF16), qk_g, bd, ln, ws, bias, sbw=sbw, sgw=sgw)
        oa = _attn_call(q, k, v)
        mod2 = jnp.stack([g1, sh2, sc2], axis=1)
        j = l // 2
        if l % 2 == 0:
            x1, h = _outproj_call(oa, ob, x, mod2, norm_ffn_g[l][None], w_out[l].astype(BF16))
            tf = _largest_divisor(ffn_w_gate.shape[-1], (512, 256, 128))
            wg = _chunk_cols(ffn_w_gate[j], tf)
            wu = _chunk_cols(ffn_w_up[j], tf)
            wd = ffn_w_down[j].reshape(-1, tf, d).astype(BF16)
            x = _ffn_call(h, x1, g2[:, None], wg, wu, wd)
        else:
            rw_t = jnp.zeros((2 * N_EXPERTS, d), BF16).at[:N_EXPERTS].set(router_w[j].T.astype(BF16))
            rb = router_b[j].reshape(N_EXPERTS, 1)
            x1, h, logits_t = _outproj_call(oa, ob, x, mod2, norm_ffn_g[l][None],
                                            w_out[l].astype(BF16), router=(rw_t, rb))
            x = _moe_layer(h, x1, g2[:, None], logits_t, moe_w_gate[j].astype(BF16),
                           moe_w_up[j].astype(BF16), moe_w_down[j].astype(BF16))
    return x
```

```python
import functools
import math

import jax
import jax.numpy as jnp
from jax import lax
from jax.experimental import pallas as pl
from jax.experimental.pallas import tpu as pltpu

F32 = jnp.float32
BF16 = jnp.bfloat16

HEAD_DIM = 64
PAIR = 2 * HEAD_DIM
SEQ_BLOCK = 128
CHUNK = 64
N_EXPERTS = 8
EPS = 1e-6
LOG2_E = 1.4426950408889634

LANES = 128
SUBLANES = 8
MXU_DIM = 256
VMEM_LIMIT_BYTES = 56 * 1024 * 1024

ATT_TILE = MXU_DIM
ATT_PAIRS = 4
TM_PROJ = 1024
TM_FFN = 512
TM_MOE = 1024
TF_MOE = 1792
FFN_SUB = 2
TT_ROWS = 512


def _cparams(sem):
    return pltpu.CompilerParams(dimension_semantics=sem, vmem_limit_bytes=VMEM_LIMIT_BYTES)


def _split_bf16(x):
    hi = x.astype(BF16)
    lo = (x - hi.astype(F32)).astype(BF16)
    return hi, lo


def _largest_divisor(n, candidates):
    for c in candidates:
        if n % c == 0:
            return c
    raise ValueError(f"no tile in {candidates} divides {n}")


def _ada_kernel(c_ref, w_ref, b_ref, o_ref):
    c = c_ref[...]
    c_act = (c * jax.nn.sigmoid(c)).astype(BF16)
    o_ref[0] = jnp.dot(c_act, w_ref[0].astype(BF16), preferred_element_type=F32) + b_ref[0]


def _ada_call(c, ada_w, ada_b):
    depth, d, n = ada_w.shape
    b = c.shape[0]
    tn = _largest_divisor(n, (1536, 1024, 512, 256, 128))
    return pl.pallas_call(
        _ada_kernel,
        grid=(depth, n // tn),
        in_specs=[
            pl.BlockSpec((b, d), lambda l, j: (0, 0)),
            pl.BlockSpec((1, d, tn), lambda l, j: (l, 0, j)),
            pl.BlockSpec((1, 1, tn), lambda l, j: (l, 0, j)),
        ],
        out_specs=pl.BlockSpec((1, b, tn), lambda l, j: (l, 0, j)),
        out_shape=jax.ShapeDtypeStruct((depth, b, n), F32),
        compiler_params=_cparams(("arbitrary", "arbitrary")),
        name="ada_ln",
    )(c, ada_w, ada_b.reshape(depth, 1, n))


def _group_mean_sq(t, bd):
    wb = bd.shape[0]
    sq = t * t
    hi, lo = _split_bf16(sq)
    cols = []
    for c0 in range(0, t.shape[1], wb):
        s = jnp.dot(hi[:, c0:c0 + wb], bd, preferred_element_type=F32)
        s = s + jnp.dot(lo[:, c0:c0 + wb], bd, preferred_element_type=F32)
        cols.append(s)
    ss = cols[0] if len(cols) == 1 else jnp.concatenate(cols, axis=1)
    return ss * (1.0 / HEAD_DIM)


def _gelu(t):
    return 0.5 * t * (1.0 + lax.erf(t * math.sqrt(0.5)))


def _inproj_kernel(x_ref, mod_ref, ng_ref, w_ref, qkg_ref, bd_ref, ln_ref, ws_ref, bias_ref,
                   q_ref, k_ref, v_ref, ob_ref, *, sbw, sgw):
    xf = x_ref[0]
    ms = jnp.mean(xf * xf, axis=-1, keepdims=True)
    y = xf * lax.rsqrt(ms + EPS) * ng_ref[...]
    h = (y * (1.0 + mod_ref[0, 1:2, :]) + mod_ref[0, 0:1, :]).astype(BF16)

    bd = bd_ref[...]
    q = jnp.dot(h, w_ref[:, 0:sbw], preferred_element_type=F32)
    qn = q * lax.rsqrt(_group_mean_sq(q, bd) + EPS) * qkg_ref[0:1, :]
    q_ref[0] = qn.astype(BF16)
    k = jnp.dot(h, w_ref[:, sbw:2 * sbw], preferred_element_type=F32)
    kn = k * lax.rsqrt(_group_mean_sq(k, bd) + EPS) * qkg_ref[1:2, :]
    k_ref[0] = kn.astype(BF16)
    v_ref[0] = jnp.dot(h, w_ref[:, 2 * sbw:3 * sbw], preferred_element_type=F32).astype(BF16)

    u = _gelu(jnp.dot(h, w_ref[:, 3 * sbw:3 * sbw + sgw], preferred_element_type=F32))
    g = _gelu(jnp.dot(h, w_ref[:, 3 * sbw + sgw:3 * sbw + 2 * sgw], preferred_element_type=F32))
    mu = jnp.mean(g, axis=-1, keepdims=True)
    gc = g - mu
    var = jnp.mean(gc * gc, axis=-1, keepdims=True)
    gn = (gc * lax.rsqrt(var + EPS) * ln_ref[0:1, :] + ln_ref[1:2, :]).astype(BF16)

    first = lax.broadcasted_iota(jnp.int32, (SEQ_BLOCK, PAIR), 1) < HEAD_DIM
    zero = jnp.zeros((SEQ_BLOCK, PAIR), BF16)
    n_blocks = xf.shape[0] // SEQ_BLOCK
    per_dot = MXU_DIM // PAIR if n_blocks % (MXU_DIM // PAIR) == 0 else 1
    for r0 in range(0, n_blocks, per_dot):
        blocks = [slice((r0 + n) * SEQ_BLOCK, (r0 + n + 1) * SEQ_BLOCK) for n in range(per_dot)]
        for p in range(sgw // PAIR):
            cols = slice(p * PAIR, (p + 1) * PAIR)
            vbs = [gn[rows, cols] for rows in blocks]
            top = jnp.concatenate([jnp.where(first, vb, zero) for vb in vbs], axis=1)
            bot = jnp.concatenate([jnp.where(first, zero, vb) for vb in vbs], axis=1)
            mixed = jnp.dot(ws_ref[p], jnp.concatenate([top, bot], axis=0), preferred_element_type=F32)
            for n, rows in enumerate(blocks):
                m = mixed[:, n * PAIR:(n + 1) * PAIR]
                ob_ref[0, rows, cols] = (u[rows, cols] * (m + bias_ref[:, cols])).astype(BF16)


def _inproj_call(x, mod, norm_g, w_in, qk_g, bd, ln, ws, bias, *, sbw, sgw):
    b, s, d = x.shape
    tm = min(TM_PROJ, s)
    n_in = w_in.shape[1]
    out = jax.ShapeDtypeStruct((b, s, sbw), BF16)
    row_spec = lambda w: pl.BlockSpec((1, tm, w), lambda i, j: (i, j, 0))
    const = lambda shape: pl.BlockSpec(shape, lambda i, j: (0,) * len(shape))
    return pl.pallas_call(
        functools.partial(_inproj_kernel, sbw=sbw, sgw=sgw),
        grid=(b, s // tm),
        in_specs=[
            row_spec(d),
            pl.BlockSpec((1, 2, d), lambda i, j: (i, 0, 0)),
            const((1, d)),
            const((d, n_in)),
            const((2, sbw)),
            const(bd.shape),
            const((2, sgw)),
            const(ws.shape),
            const((SEQ_BLOCK, sgw)),
        ],
        out_specs=[row_spec(sbw), row_spec(sbw), row_spec(sbw), row_spec(sgw)],
        out_shape=[out, out, out, jax.ShapeDtypeStruct((b, s, sgw), BF16)],
        compiler_params=_cparams(("parallel", "arbitrary")),
        name="mixer_in",
    )(x, mod, norm_g, w_in, qk_g, bd, ln, ws, bias)


def _attn_kernel(q_ref, k_ref, v_ref, o_ref, acc_ref, carry_ref, za_ref, zb_ref):
    i = pl.program_id(2)
    tq = ATT_TILE
    hb = SEQ_BLOCK
    n_sub = tq // hb
    n_pairs = q_ref.shape[2] // PAIR
    first = lax.broadcasted_iota(jnp.int32, (tq, PAIR), 1) < HEAD_DIM
    zero = jnp.zeros((tq, PAIR), BF16)
    q2 = []
    for p in range(n_pairs):
        q = q_ref[0, :, p * PAIR:(p + 1) * PAIR]
        q2.append(jnp.concatenate([jnp.where(first, q, zero), jnp.where(first, zero, q)], axis=0))

    kk = lax.broadcasted_iota(jnp.int32, (2 * hb, 2 * hb), 0) & (hb - 1)
    nn = lax.broadcasted_iota(jnp.int32, (2 * hb, 2 * hb), 1)
    csum = jnp.where((kk >= nn) | (nn >= hb), -1.0, 0.0).astype(BF16)

    qpos = lax.broadcasted_iota(jnp.int32, (2 * tq, tq), 0) & (tq - 1)
    kpos = lax.broadcasted_iota(jnp.int32, (2 * tq, tq), 1)
    strict = kpos < qpos

    def scores(j, p):
        start = pl.multiple_of(j * tq, tq)
        kb = k_ref[0, pl.ds(start, tq), p * PAIR:(p + 1) * PAIR]
        return lax.dot_general(q2[p], kb, (((1,), (1,)), ((), ())), preferred_element_type=F32)

    def sweep(j, z_of, diagonal, ahead=None):
        start = pl.multiple_of(j * tq, tq)
        for p in range(n_pairs):
            z = z_of(p)
            sp = jnp.maximum(z, 0.0) + jnp.log(1.0 + jnp.exp2(jnp.abs(z) * (-LOG2_E)))
            if diagonal:
                sp = jnp.where(strict, sp, 0.0)
            hi, lo = _split_bf16(sp)
            carry = None if diagonal else carry_ref[p]
            la = [None] * n_sub
            for sb in reversed(range(n_sub)):
                cols = slice(sb * hb, (sb + 1) * hb)
                lhs = jnp.concatenate([hi[:, cols], lo[:, cols]], axis=1)
                cs = jnp.dot(lhs, csum, preferred_element_type=F32)
                la[sb] = cs[:, :hb] if carry is None else cs[:, :hb] + carry
                carry = cs[:, hb:] if carry is None else carry + cs[:, hb:]
            carry_ref[p] = carry
            w = jnp.exp(z + jnp.concatenate(la, axis=1))
            if diagonal:
                w = jnp.where(strict, w, 0.0)
            vb = v_ref[0, pl.ds(start, tq), p * PAIR:(p + 1) * PAIR]
            pv = jnp.dot(w.astype(BF16), vb, preferred_element_type=F32)
            if diagonal:
                acc_ref[p] = pv
            else:
                acc_ref[p] += pv
            if ahead is not None:
                ahead[0][p] = scores(ahead[1], p)

    sweep(i, lambda p: scores(i, p), True, ahead=(za_ref, jnp.maximum(i - 1, 0)))

    def body(t, c):
        j = i - 1 - 2 * t
        sweep(j, lambda p: za_ref[p], False, ahead=(zb_ref, j - 1))
        sweep(j - 1, lambda p: zb_ref[p], False, ahead=(za_ref, jnp.maximum(j - 2, 0)))
        return c

    lax.fori_loop(0, i // 2, body, 0)

    @pl.when(i % 2 == 1)
    def _():
        sweep(0, lambda p: za_ref[p], False)
    for p in range(n_pairs):
        o_ref[0, :, p * PAIR:(p + 1) * PAIR] = jnp.where(
            first, acc_ref[p, 0:tq, :], acc_ref[p, tq:2 * tq, :]).astype(BF16)


def _attn_call(q, k, v):
    b, s, w = q.shape
    tq = ATT_TILE
    wl = ATT_PAIRS * PAIR if w % (ATT_PAIRS * PAIR) == 0 else PAIR
    assert s % tq == 0
    return pl.pallas_call(
        _attn_kernel,
        grid=(b, w // wl, s // tq),
        in_specs=[
            pl.BlockSpec((1, tq, wl), lambda bi, p, i: (bi, i, p)),
            pl.BlockSpec((1, s, wl), lambda bi, p, i: (bi, 0, p)),
            pl.BlockSpec((1, s, wl), lambda bi, p, i: (bi, 0, p)),
        ],
        out_specs=pl.BlockSpec((1, tq, wl), lambda bi, p, i: (bi, i, p)),
        out_shape=jax.ShapeDtypeStruct((b, s, w), BF16),
        scratch_shapes=[pltpu.VMEM((wl // PAIR, 2 * tq, PAIR), F32),
                        pltpu.VMEM((wl // PAIR, 2 * tq, SEQ_BLOCK), F32),
                        pltpu.VMEM((wl // PAIR, 2 * tq, tq), F32),
                        pltpu.VMEM((wl // PAIR, 2 * tq, tq), F32)],
        compiler_params=_cparams(("parallel", "parallel", "arbitrary")),
        name="sb_attn",
    )(q, k, v)


def _outproj_kernel(*refs, sbw, with_router):
    if with_router:
        oa_ref, ob_ref, x_ref, mod_ref, ng_ref, w_ref, rw_ref, rb_ref, x1_ref, h_ref, lg_ref = refs
    else:
        oa_ref, ob_ref, x_ref, mod_ref, ng_ref, w_ref, x1_ref, h_ref = refs
    y = jnp.dot(oa_ref[0], w_ref[0:sbw, :], preferred_element_type=F32)
    y = y + jnp.dot(ob_ref[0], w_ref[sbw:, :], preferred_element_type=F32)
    x1 = x_ref[0] + mod_ref[0, 0:1, :] * y
    x1_ref[0] = x1
    ms = jnp.mean(x1 * x1, axis=-1, keepdims=True)
    hn = x1 * lax.rsqrt(ms + EPS) * ng_ref[...]
    h = (hn * (1.0 + mod_ref[0, 2:3, :]) + mod_ref[0, 1:2, :]).astype(BF16)
    h_ref[0] = h
    if with_router:
        lg = lax.dot_general(rw_ref[...], h, (((1,), (1,)), ((), ())), preferred_element_type=F32)
        lg_ref[...] = lg[0:N_EXPERTS, :] + rb_ref[...]


def _outproj_call(oa, ob, x, mod, norm_g, w_out, router=None):
    b, s, d = x.shape
    sbw = oa.shape[-1]
    tm = min(TM_PROJ, s)
    row_spec = lambda w: pl.BlockSpec((1, tm, w), lambda i, j: (i, j, 0))
    const = lambda shape: pl.BlockSpec(shape, lambda i, j: (0,) * len(shape))
    in_specs = [row_spec(sbw), row_spec(ob.shape[-1]), row_spec(d),
                pl.BlockSpec((1, 3, d), lambda i, j: (i, 0, 0)), const((1, d)), const(w_out.shape)]
    out_specs = [row_spec(d), row_spec(d)]
    out_shape = [jax.ShapeDtypeStruct((b, s, d), F32), jax.ShapeDtypeStruct((b, s, d), BF16)]
    args = [oa, ob, x, mod, norm_g, w_out]
    if router is not None:
        rw_t, rb = router
        in_specs += [const(rw_t.shape), const(rb.shape)]
        nj = s // tm
        out_specs.append(pl.BlockSpec((N_EXPERTS, tm), lambda i, j: (0, i * nj + j)))
        out_shape.append(jax.ShapeDtypeStruct((N_EXPERTS, b * s), F32))
        args += [rw_t, rb]
    return pl.pallas_call(
        functools.partial(_outproj_kernel, sbw=sbw, with_router=router is not None),
        grid=(b, s // tm),
        in_specs=in_specs,
        out_specs=out_specs,
        out_shape=out_shape,
        compiler_params=_cparams(("parallel", "arbitrary")),
        name="mixer_out",
    )(*args)


def _swiglu_partial(h, wg_ref, wu_ref, wd_ref, parts):
    out = None
    for gcols, drows in parts:
        g = jnp.dot(h, wg_ref[gcols], preferred_element_type=F32)
        u = jnp.dot(h, wu_ref[gcols], preferred_element_type=F32)
        a = (g * jax.nn.sigmoid(g) * u).astype(BF16)
        y = jnp.dot(a, wd_ref[drows], preferred_element_type=F32)
        out = y if out is None else out + y
    return out


def _ffn_kernel(h_ref, x_ref, g2_ref, wg_ref, wu_ref, wd_ref, o_ref, acc_ref):
    h = h_ref[0]
    n_chunks = wg_ref.shape[0]
    for c0 in range(0, n_chunks, FFN_SUB):
        parts = [((c,), (c,)) for c in range(c0, min(c0 + FFN_SUB, n_chunks))]
        y = _swiglu_partial(h, wg_ref, wu_ref, wd_ref, parts)
        if c0 == 0:
            acc_ref[...] = y
        else:
            acc_ref[...] += y
    o_ref[0] = x_ref[0] + g2_ref[0] * acc_ref[...]


def _ffn_call(h, x, g2, wg, wu, wd):
    b, s, d = x.shape
    tm = min(TM_FFN, s)
    row_spec = pl.BlockSpec((1, tm, d), lambda i, j: (i, j, 0))
    resident = lambda a: pl.BlockSpec(a.shape, lambda i, j: (0, 0, 0), pipeline_mode=pl.Buffered(1))
    return pl.pallas_call(
        _ffn_kernel,
        grid=(b, s // tm),
        in_specs=[row_spec, row_spec, pl.BlockSpec((1, 1, d), lambda i, j: (i, 0, 0)),
                  resident(wg), resident(wu), resident(wd)],
        out_specs=row_spec,
        out_shape=jax.ShapeDtypeStruct((b, s, d), F32),
        scratch_shapes=[pltpu.VMEM((tm, d), F32)],
        compiler_params=_cparams(("parallel", "arbitrary")),
        name="dense_ffn",
    )(h, x, g2, wg, wu, wd)


def _route_kernel(lg_ref, oi_ref, of_ref, tile_ref, cnt_ref, carry_ref):
    step = pl.program_id(0)
    tl = lg_ref.shape[1]
    ne = N_EXPERTS

    @pl.when(step == 0)
    def _():
        carry_ref[...] = jnp.zeros_like(carry_ref)

    lg = lg_ref[...]
    eidx = lax.broadcasted_iota(jnp.int32, (ne, tl), 0).astype(F32)
    m1 = jnp.max(lg, axis=0, keepdims=True)
    i1 = jnp.min(jnp.where(lg == m1, eidx, float(ne)), axis=0, keepdims=True)
    lg2 = jnp.where(eidx == i1, -jnp.inf, lg)
    m2 = jnp.max(lg2, axis=0, keepdims=True)
    i2 = jnp.min(jnp.where(lg2 == m2, eidx, float(ne)), axis=0, keepdims=True)
    t = jnp.exp(m2 - m1)
    gate1 = 1.0 / (1.0 + t)
    gate2 = t / (1.0 + t)

    sel1 = eidx == i1
    sel2 = eidx == i2
    onehot = jnp.where(sel1 | sel2, 1.0, 0.0)
    a = lax.broadcasted_iota(jnp.int32, (tl, tl), 0)
    bcol = lax.broadcasted_iota(jnp.int32, (tl, tl), 1)
    upper = jnp.where(a < bcol, 1.0, 0.0).astype(BF16)
    lhs = jnp.concatenate([onehot, jnp.zeros_like(onehot)], axis=0).astype(BF16)
    excl_tile = jnp.dot(lhs, upper, preferred_element_type=F32)[0:ne, :]
    before = carry_ref[...]
    n_tile = jnp.sum(onehot, axis=1, keepdims=True) + jnp.zeros_like(before)
    n_tile = jnp.floor((n_tile + (SUBLANES - 1)) * (1.0 / SUBLANES)) * SUBLANES
    off = jnp.concatenate([jnp.zeros((1, LANES), F32)] +
                          [jnp.sum(n_tile[0:e], axis=0, keepdims=True) for e in range(1, ne)], axis=0)
    pick = lambda sel, v: jnp.sum(jnp.where(sel, v, 0.0), axis=0, keepdims=True)
    r1 = pick(sel1, excl_tile + before[:, 0:1])
    r2 = pick(sel2, excl_tile + before[:, 0:1])
    p1 = pick(sel1, excl_tile + off[:, 0:1])
    p2 = pick(sel2, excl_tile + off[:, 0:1])
    carry_ref[...] = before + n_tile

    as_i = lambda v: v.astype(jnp.int32)
    oi_ref[...] = jnp.concatenate(
        [as_i(i1), as_i(i2), as_i(r1), as_i(r2), as_i(p1), as_i(p2), jnp.zeros((ne - 6, tl), jnp.int32)],
        axis=0)
    of_ref[...] = jnp.concatenate([gate1, gate2, jnp.zeros((ne - 2, tl), F32)], axis=0)
    lane = lax.broadcasted_iota(jnp.int32, (ne, LANES), 1)
    tile_ref[0] = as_i(jnp.where(lane == 0, n_tile, jnp.where(lane == 1, before, 0.0)))
    cnt_ref[...] = carry_ref[...]


def _route_call(logits_t):
    ne, t = logits_t.shape
    tl = min(TT_ROWS, t)
    blk = pl.BlockSpec((ne, tl), lambda i: (0, i))
    return pl.pallas_call(
        _route_kernel,
        grid=(t // tl,),
        in_specs=[blk],
        out_specs=[blk, blk, pl.BlockSpec((1, ne, LANES), lambda i: (i, 0, 0)),
                   pl.BlockSpec((ne, LANES), lambda i: (0, 0))],
        out_shape=[jax.ShapeDtypeStruct((ne, t), jnp.int32), jax.ShapeDtypeStruct((ne, t), F32),
                   jax.ShapeDtypeStruct((t // tl, ne, LANES), jnp.int32),
                   jax.ShapeDtypeStruct((ne, LANES), F32)],
        scratch_shapes=[pltpu.VMEM((ne, LANES), F32)],
        compiler_params=_cparams(("arbitrary",)),
        name="moe_route",
    )(logits_t)


def _run_pieces(n, max_rows):
    size = max_rows
    while size >= SUBLANES:
        yield size, pl.multiple_of(n & ~(2 * size - 1), SUBLANES), (n & size) != 0
        size //= 2


def _compact_rows(tt):
    return 2 * tt + N_EXPERTS * SUBLANES


def _dispatch_kernel(tail_ref, cnt_ref, base_ref, pos_ref, h_ref, xb_ref, c_ref, zero_ref, sem, zsem):
    step = pl.program_id(0)
    tt = h_ref.shape[0]
    tmb = zero_ref.shape[0]

    def zero_block(blk):
        row = pl.multiple_of(blk * tmb, tmb)
        return pltpu.make_async_copy(zero_ref, xb_ref.at[pl.ds(row, tmb)], zsem)

    @pl.when(step == 0)
    def _():
        zero_ref[...] = jnp.zeros_like(zero_ref)
        for e in range(N_EXPERTS):
            zero_block(tail_ref[e]).start()
        for e in range(N_EXPERTS):
            zero_block(tail_ref[e]).wait()

        def zero_unused(blk, c):
            cp = zero_block(blk)
            cp.start()
            cp.wait()
            return c

        lax.fori_loop(tail_ref[N_EXPERTS], xb_ref.shape[0] // tmb, zero_unused, 0)

    slot = step % 2
    r = lax.broadcasted_iota(jnp.int32, (c_ref.shape[1], tt), 0)
    onehot = jnp.where((r == pos_ref[4:5, :]) | (r == pos_ref[5:6, :]), 1.0, 0.0).astype(BF16)
    c_ref[slot] = jnp.dot(onehot, h_ref[...], preferred_element_type=F32)

    def run_copies(op, tile, buf):
        off = 0
        for e in range(N_EXPERTS):
            n = cnt_ref[tile * N_EXPERTS + e]
            dst = pl.multiple_of(base_ref[tile * N_EXPERTS + e], SUBLANES)
            for size, piece, present in _run_pieces(n, tt):
                @pl.when(present)
                def _():
                    src = pl.multiple_of(off + piece, SUBLANES)
                    op(pltpu.make_async_copy(c_ref.at[buf, pl.ds(src, size)],
                                             xb_ref.at[pl.ds(pl.multiple_of(dst + piece, SUBLANES), size)],
                                             sem.at[buf]))
            off = off + n

    run_copies(lambda cp: cp.start(), step, slot)

    @pl.when(step > 0)
    def _():
        run_copies(lambda cp: cp.wait(), step - 1, 1 - slot)

    @pl.when(step == pl.num_programs(0) - 1)
    def _():
        run_copies(lambda cp: cp.wait(), step, slot)


def _dispatch_call(tail_blocks, tile_cnt, tile_base, pos, h, n_slots):
    t, d = h.shape
    tt = min(TT_ROWS, t)
    tmb = TM_MOE
    return pl.pallas_call(
        _dispatch_kernel,
        grid_spec=pltpu.PrefetchScalarGridSpec(
            num_scalar_prefetch=3,
            grid=(t // tt,),
            in_specs=[pl.BlockSpec((N_EXPERTS, tt), lambda i, *_: (0, i)),
                      pl.BlockSpec((tt, d), lambda i, *_: (i, 0))],
            out_specs=pl.BlockSpec(memory_space=pl.ANY),
            scratch_shapes=[pltpu.VMEM((2, _compact_rows(tt), d), F32), pltpu.VMEM((tmb, d), F32),
                            pltpu.SemaphoreType.DMA((2,)), pltpu.SemaphoreType.DMA],
        ),
        out_shape=jax.ShapeDtypeStruct((n_slots, d), F32),
        compiler_params=_cparams(("arbitrary",)),
        name="moe_dispatch",
    )(tail_blocks, tile_cnt, tile_base, pos, h)


def _moe_ffn_kernel(be_ref, nb_ref, x_ref, wg_ref, wu_ref, wd_ref, o_ref, xb_ref):
    i = pl.program_id(0)
    c = pl.program_id(1)
    used = i < nb_ref[0]
    tf = wg_ref.shape[2]
    wpart = min(tf, MXU_DIM)
    parts = [((0, slice(None), slice(p, p + wpart)), (0, slice(p, p + wpart), slice(None)))
             for p in range(0, tf, wpart)]

    @pl.when(used & (c == 0))
    def _():
        xb_ref[...] = x_ref[...].astype(BF16)
        o_ref[...] = _swiglu_partial(xb_ref[...], wg_ref, wu_ref, wd_ref, parts)

    @pl.when(used & (c != 0))
    def _():
        o_ref[...] += _swiglu_partial(xb_ref[...], wg_ref, wu_ref, wd_ref, parts)

    @pl.when(jnp.logical_not(used) & (c == 0))
    def _():
        o_ref[...] = jnp.zeros_like(o_ref)


def _moe_ffn_call(block_e, n_used, xb, wg, wu, wd):
    n_slots, d = xb.shape
    tmb = TM_MOE
    nb = n_slots // tmb
    f = wg.shape[2]
    tf = _largest_divisor(f, (TF_MOE, 512, 256, 128))
    nc = f // tf
    blk = lambda i, nbu: jnp.minimum(i, nbu[0] - 1)
    chunk = lambda i, c, nbu: jnp.where(i < nbu[0], c, nc - 1)
    return pl.pallas_call(
        _moe_ffn_kernel,
        grid_spec=pltpu.PrefetchScalarGridSpec(
            num_scalar_prefetch=2,
            grid=(nb, nc),
            in_specs=[
                pl.BlockSpec((tmb, d), lambda i, c, be, nbu: (blk(i, nbu), 0)),
                pl.BlockSpec((1, d, tf), lambda i, c, be, nbu: (be[blk(i, nbu)], 0, chunk(i, c, nbu))),
                pl.BlockSpec((1, d, tf), lambda i, c, be, nbu: (be[blk(i, nbu)], 0, chunk(i, c, nbu))),
                pl.BlockSpec((1, tf, d), lambda i, c, be, nbu: (be[blk(i, nbu)], chunk(i, c, nbu), 0)),
            ],
            out_specs=pl.BlockSpec((tmb, d), lambda i, c, be, nbu: (i, 0)),
            scratch_shapes=[pltpu.VMEM((tmb, d), BF16)],
        ),
        out_shape=jax.ShapeDtypeStruct((n_slots, d), F32),
        compiler_params=_cparams(("arbitrary", "arbitrary")),
        name="moe_ffn",
    )(block_e, n_used, xb, wg, wu, wd)


def _combine_kernel(dest_ref, yb_ref, x_ref, g2_ref, gate_ref, o_ref, rows_ref, sem):
    tt = x_ref.shape[1]

    def row_copy(t, k):
        return pltpu.make_async_copy(yb_ref.at[pl.ds(dest_ref[0, k, t], 1)],
                                     rows_ref.at[k, pl.ds(t, 1)], sem)

    def wait(t, c):
        row_copy(t, 0).wait()
        row_copy(t, 1).wait()
        return c

    for t in range(tt):
        row_copy(t, 0).start(priority=0)
        row_copy(t, 1).start(priority=1)
    lax.fori_loop(0, tt, wait, 0, unroll=8)
    y = rows_ref[0] * gate_ref[0, :, 0:1] + rows_ref[1] * gate_ref[0, :, 1:2]
    o_ref[0] = x_ref[0] + g2_ref[0] * y


def _combine_call(dest, yb, x, g2, gates):
    b, s, d = x.shape
    tt = dest.shape[2]
    nj = s // tt
    row_spec = pl.BlockSpec((1, tt, d), lambda i, j: (i, j, 0))
    return pl.pallas_call(
        _combine_kernel,
        grid=(b, nj),
        in_specs=[
            pl.BlockSpec((1, 2, tt), lambda i, j: (i * nj + j, 0, 0), memory_space=pltpu.SMEM),
            pl.BlockSpec(memory_space=pl.ANY),
            row_spec,
            pl.BlockSpec((1, 1, d), lambda i, j: (i, 0, 0)),
            pl.BlockSpec((1, tt, 2), lambda i, j: (i * nj + j, 0, 0)),
        ],
        out_specs=row_spec,
        out_shape=jax.ShapeDtypeStruct((b, s, d), F32),
        scratch_shapes=[pltpu.VMEM((2, tt, d), F32), pltpu.SemaphoreType.DMA],
        compiler_params=_cparams(("arbitrary", "arbitrary")),
        name="moe_combine",
    )(dest, yb, x, g2, gates)


def _chunk_cols(w, tf):
    d, f = w.shape
    return w.reshape(d, f // tf, tf).transpose(1, 0, 2).astype(BF16)


def _moe_layer(h, x1, g2, logits_t, wg, wu, wd):
    b, s, d = x1.shape
    t = b * s
    tmb = TM_MOE
    tt = min(TT_ROWS, s)
    ne = N_EXPERTS
    oi, of, tiles, cnt = _route_call(logits_t)
    e1, e2, r1, r2 = oi[0], oi[1], oi[2], oi[3]
    counts = cnt[:, 0].astype(jnp.int32)
    padded = (counts + tmb - 1) // tmb * tmb
    pad_end = jnp.cumsum(padded)
    pad_start = pad_end - padded
    n_used = (pad_end[-1] // tmb).astype(jnp.int32)
    nb = -(-(2 * t + (t // tt) * ne * (SUBLANES - 1)) // tmb) + ne
    block_row0 = jnp.arange(nb, dtype=jnp.int32) * tmb
    block_e = jnp.minimum(jnp.sum(pad_end[None, :] <= block_row0[:, None], axis=1), ne - 1).astype(jnp.int32)
    tail_blocks = jnp.where(padded > 0, pad_end // tmb - 1, nb - 1)
    tail_blocks = jnp.concatenate([tail_blocks, n_used[None]]).astype(jnp.int32)
    start_of = lambda e: jnp.sum(jnp.where(e[None, :] == jnp.arange(ne)[:, None], pad_start[:, None], 0), axis=0)
    d1 = start_of(e1) + r1
    d2 = start_of(e2) + r2
    dest = jnp.stack([d1.reshape(t // tt, tt), d2.reshape(t // tt, tt)], axis=1).astype(jnp.int32)

    tile_cnt = tiles[:, :, 0].reshape(-1)
    tile_base = (tiles[:, :, 1] + pad_start[None, :]).reshape(-1).astype(jnp.int32)
    xb = _dispatch_call(tail_blocks, tile_cnt, tile_base, oi, h.reshape(t, d), nb * tmb)
    yb = _moe_ffn_call(block_e, n_used.reshape(1), xb, wg, wu, wd)
    gates = jnp.stack([of[0], of[1]], axis=-1).reshape(t // tt, tt, 2)
    return _combine_call(dest, yb, x1, g2, gates)


def kernel(x, c, ada_w, ada_b, norm_mix_g, norm_ffn_g, w_in, q_norm_g, k_norm_g, sg_ln_g, sg_ln_b,
           sg_w_spatial, sg_b_spatial, w_out, ffn_w_gate, ffn_w_up, ffn_w_down, router_w, router_b,
           moe_w_gate, moe_w_up, moe_w_down):
    b, s, d = x.shape
    depth = ada_w.shape[0]
    sgw = sg_ln_g.shape[-1]
    sbw = w_out.shape[1] - sgw
    assert sbw % PAIR == 0 and sgw % PAIR == 0 and s % SEQ_BLOCK == 0
    assert w_in.shape[-1] == 3 * sbw + 2 * sgw

    ada = _ada_call(c, ada_w, ada_b)
    ada = ada.reshape(depth, b, 6, d)

    wb = MXU_DIM if sbw % MXU_DIM == 0 else PAIR
    gi = jnp.arange(wb) // HEAD_DIM
    bd = (gi[:, None] == gi[None, :]).astype(BF16)
    pos = jnp.arange(SEQ_BLOCK)
    chunk_causal = (pos[None, :] // CHUNK) <= (pos[:, None] // CHUNK)
    scale = HEAD_DIM ** -0.5

    for l in range(depth):
        sh1, sc1, g1, sh2, sc2, g2 = (ada[l, :, j] for j in range(6))
        qk_g = jnp.stack([jnp.tile(q_norm_g[l], sbw // HEAD_DIM) * scale,
                          jnp.tile(k_norm_g[l], sbw // HEAD_DIM)])
        ln = jnp.stack([sg_ln_g[l], sg_ln_b[l]])
        ws = jnp.where(chunk_causal[None], sg_w_spatial[l], 0.0)
        ws = ws.reshape(sgw // PAIR, 2, SEQ_BLOCK, SEQ_BLOCK).transpose(0, 2, 1, 3)
        ws = ws.reshape(sgw // PAIR, SEQ_BLOCK, 2 * SEQ_BLOCK).astype(BF16)
        bias = jnp.repeat(sg_b_spatial[l].T, HEAD_DIM, axis=1)

        q, k, v, ob = _inproj_call(x, jnp.stack([sh1, sc1], axis=1), norm_mix_g[l][None],
                                   w_in[l].astype(BF16), qk_g, bd, ln, ws, bias, sbw=sbw, sgw=sgw)
        oa = _attn_call(q, k, v)
        mod2 = jnp.stack([g1, sh2, sc2], axis=1)
        j = l // 2
        if l % 2 == 0:
            x1, h = _outproj_call(oa, ob, x, mod2, norm_ffn_g[l][None], w_out[l].astype(BF16))
            tf = _largest_divisor(ffn_w_gate.shape[-1], (512, 256, 128))
            wg = _chunk_cols(ffn_w_gate[j], tf)
            wu = _chunk_cols(ffn_w_up[j], tf)
            wd = ffn_w_down[j].reshape(-1, tf, d).astype(BF16)
            x = _ffn_call(h, x1, g2[:, None], wg, wu, wd)
        else:
            rw_t = jnp.zeros((2 * N_EXPERTS, d), BF16).at[:N_EXPERTS].set(router_w[j].T.astype(BF16))
            rb = router_b[j].reshape(N_EXPERTS, 1)
            x1, h, logits_t = _outproj_call(oa, ob, x, mod2, norm_ffn_g[l][None],
                                            w_out[l].astype(BF16), router=(rw_t, rb))
            x = _moe_layer(h, x1, g2[:, None], logits_t, moe_w_gate[j].astype(BF16),
                           moe_w_up[j].astype(BF16), moe_w_down[j].astype(BF16))
    return x
```

```python
import functools
import math

import jax
import jax.numpy as jnp
from jax import lax
from jax.experimental import pallas as pl
from jax.experimental.pallas import tpu as pltpu

F32 = jnp.float32
BF16 = jnp.bfloat16

HEAD_DIM = 64
PAIR = 2 * HEAD_DIM
SEQ_BLOCK = 128
CHUNK = 64
N_EXPERTS = 8
EPS = 1e-6
LOG2_E = 1.4426950408889634

LANES = 128
SUBLANES = 8
MXU_DIM = 256
VMEM_LIMIT_BYTES = 56 * 1024 * 1024

ATT_TILE = MXU_DIM
ATT_PAIRS = 4
TM_PROJ = 1024
TM_FFN = 512
TM_MOE = 1024
TF_MOE = 1792
FFN_SUB = 2
TT_ROWS = 512


def _cparams(sem):
    return pltpu.CompilerParams(dimension_semantics=sem, vmem_limit_bytes=VMEM_LIMIT_BYTES)


def _split_bf16(x):
    hi = x.astype(BF16)
    lo = (x - hi.astype(F32)).astype(BF16)
    return hi, lo


def _largest_divisor(n, candidates):
    for c in candidates:
        if n % c == 0:
            return c
    raise ValueError(f"no tile in {candidates} divides {n}")


def _ada_kernel(c_ref, w_ref, b_ref, o_ref):
    c = c_ref[...]
    c_act = (c * jax.nn.sigmoid(c)).astype(BF16)
    o_ref[0] = jnp.dot(c_act, w_ref[0].astype(BF16), preferred_element_type=F32) + b_ref[0]


def _ada_call(c, ada_w, ada_b):
    depth, d, n = ada_w.shape
    b = c.shape[0]
    tn = _largest_divisor(n, (1536, 1024, 512, 256, 128))
    return pl.pallas_call(
        _ada_kernel,
        grid=(depth, n // tn),
        in_specs=[
            pl.BlockSpec((b, d), lambda l, j: (0, 0)),
            pl.BlockSpec((1, d, tn), lambda l, j: (l, 0, j)),
            pl.BlockSpec((1, 1, tn), lambda l, j: (l, 0, j)),
        ],
        out_specs=pl.BlockSpec((1, b, tn), lambda l, j: (l, 0, j)),
        out_shape=jax.ShapeDtypeStruct((depth, b, n), F32),
        compiler_params=_cparams(("arbitrary", "arbitrary")),
        name="ada_ln",
    )(c, ada_w, ada_b.reshape(depth, 1, n))


def _group_mean_sq(t, bd):
    wb = bd.shape[0]
    sq = t * t
    hi, lo = _split_bf16(sq)
    cols = []
    for c0 in range(0, t.shape[1], wb):
        s = jnp.dot(hi[:, c0:c0 + wb], bd, preferred_element_type=F32)
        s = s + jnp.dot(lo[:, c0:c0 + wb], bd, preferred_element_type=F32)
        cols.append(s)
    ss = cols[0] if len(cols) == 1 else jnp.concatenate(cols, axis=1)
    return ss * (1.0 / HEAD_DIM)


def _gelu(t):
    return 0.5 * t * (1.0 + lax.erf(t * math.sqrt(0.5)))


def _inproj_kernel(x_ref, mod_ref, ng_ref, w_ref, qkg_ref, bd_ref, ln_ref, ws_ref, bias_ref,
                   q_ref, k_ref, v_ref, ob_ref, *, sbw, sgw):
    xf = x_ref[0]
    ms = jnp.mean(xf * xf, axis=-1, keepdims=True)
    y = xf * lax.rsqrt(ms + EPS) * ng_ref[...]
    h = (y * (1.0 + mod_ref[0, 1:2, :]) + mod_ref[0, 0:1, :]).astype(BF16)

    bd = bd_ref[...]
    q = jnp.dot(h, w_ref[:, 0:sbw], preferred_element_type=F32)
    qn = q * lax.rsqrt(_group_mean_sq(q, bd) + EPS) * qkg_ref[0:1, :]
    q_ref[0] = qn.astype(BF16)
    k = jnp.dot(h, w_ref[:, sbw:2 * sbw], preferred_element_type=F32)
    kn = k * lax.rsqrt(_group_mean_sq(k, bd) + EPS) * qkg_ref[1:2, :]
    k_ref[0] = kn.astype(BF16)
    v_ref[0] = jnp.dot(h, w_ref[:, 2 * sbw:3 * sbw], preferred_element_type=F32).astype(BF16)

    u = _gelu(jnp.dot(h, w_ref[:, 3 * sbw:3 * sbw + sgw], preferred_element_type=F32))
    g = _gelu(jnp.dot(h, w_ref[:, 3 * sbw + sgw:3 * sbw + 2 * sgw], preferred_element_type=F32))
    mu = jnp.mean(g, axis=-1, keepdims=True)
    gc = g - mu
    var = jnp.mean(gc * gc, axis=-1, keepdims=True)
    gn = (gc * lax.rsqrt(var + EPS) * ln_ref[0:1, :] + ln_ref[1:2, :]).astype(BF16)

    first = lax.broadcasted_iota(jnp.int32, (SEQ_BLOCK, PAIR), 1) < HEAD_DIM
    zero = jnp.zeros((SEQ_BLOCK, PAIR), BF16)
    n_blocks = xf.shape[0] // SEQ_BLOCK
    per_dot = MXU_DIM // PAIR if n_blocks % (MXU_DIM // PAIR) == 0 else 1
    for r0 in range(0, n_blocks, per_dot):
        blocks = [slice((r0 + n) * SEQ_BLOCK, (r0 + n + 1) * SEQ_BLOCK) for n in range(per_dot)]
        for p in range(sgw // PAIR):
            cols = slice(p * PAIR, (p + 1) * PAIR)
            vbs = [gn[rows, cols] for rows in blocks]
            top = jnp.concatenate([jnp.where(first, vb, zero) for vb in vbs], axis=1)
            bot = jnp.concatenate([jnp.where(first, zero, vb) for vb in vbs], axis=1)
            mixed = jnp.dot(ws_ref[p], jnp.concatenate([top, bot], axis=0), preferred_element_type=F32)
            for n, rows in enumerate(blocks):
                m = mixed[:, n * PAIR:(n + 1) * PAIR]
                ob_ref[0, rows, cols] = (u[rows, cols] * (m + bias_ref[:, cols])).astype(BF16)


def _inproj_call(x, mod, norm_g, w_in, qk_g, bd, ln, ws, bias, *, sbw, sgw):
    b, s, d = x.shape
    tm = min(TM_PROJ, s)
    n_in = w_in.shape[1]
    out = jax.ShapeDtypeStruct((b, s, sbw), BF16)
    row_spec = lambda w: pl.BlockSpec((1, tm, w), lambda i, j: (i, j, 0))
    const = lambda shape: pl.BlockSpec(shape, lambda i, j: (0,) * len(shape))
    return pl.pallas_call(
        functools.partial(_inproj_kernel, sbw=sbw, sgw=sgw),
        grid=(b, s // tm),
        in_specs=[
            row_spec(d),
            pl.BlockSpec((1, 2, d), lambda i, j: (i, 0, 0)),
            const((1, d)),
            const((d, n_in)),
            const((2, sbw)),
            const(bd.shape),
            const((2, sgw)),
            const(ws.shape),
            const((SEQ_BLOCK, sgw)),
        ],
        out_specs=[row_spec(sbw), row_spec(sbw), row_spec(sbw), row_spec(sgw)],
        out_shape=[out, out, out, jax.ShapeDtypeStruct((b, s, sgw), BF16)],
        compiler_params=_cparams(("parallel", "arbitrary")),
        name="mixer_in",
    )(x, mod, norm_g, w_in, qk_g, bd, ln, ws, bias)


def _attn_kernel(q_ref, k_ref, v_ref, o_ref, acc_ref, carry_ref, za_ref, zb_ref):
    i = pl.program_id(2)
    tq = ATT_TILE
    hb = SEQ_BLOCK
    n_sub = tq // hb
    n_pairs = q_ref.shape[2] // PAIR
    first = lax.broadcasted_iota(jnp.int32, (tq, PAIR), 1) < HEAD_DIM
    zero = jnp.zeros((tq, PAIR), BF16)
    q2 = []
    for p in range(n_pairs):
        q = q_ref[0, :, p * PAIR:(p + 1) * PAIR]
        q2.append(jnp.concatenate([jnp.where(first, q, zero), jnp.where(first, zero, q)], axis=0))

    kk = lax.broadcasted_iota(jnp.int32, (2 * hb, 2 * hb), 0) & (hb - 1)
    nn = lax.broadcasted_iota(jnp.int32, (2 * hb, 2 * hb), 1)
    csum = jnp.where((kk >= nn) | (nn >= hb), -1.0, 0.0).astype(BF16)

    qpos = lax.broadcasted_iota(jnp.int32, (2 * tq, tq), 0) & (tq - 1)
    kpos = lax.broadcasted_iota(jnp.int32, (2 * tq, tq), 1)
    strict = kpos < qpos

    def scores(j, p):
        start = pl.multiple_of(j * tq, tq)
        kb = k_ref[0, pl.ds(start, tq), p * PAIR:(p + 1) * PAIR]
        return lax.dot_general(q2[p], kb, (((1,), (1,)), ((), ())), preferred_element_type=F32)

    def sweep(j, z_ref, ahead=None):
        start = pl.multiple_of(j * tq, tq)
        for p in range(n_pairs):
            z = z_ref[p]
            sp = softplus(z)
            carry = carry_ref[p]
            la = [None] * n_sub
            for sb in reversed(range(n_sub)):
                cs = cumsum_dot(sp[:, sb * hb:(sb + 1) * hb])
                la[sb] = cs[:, :hb] + carry
                carry = carry + cs[:, hb:]
            carry_ref[p] = carry
            w = jnp.exp(z + jnp.concatenate(la, axis=1))
            vb = v_ref[0, pl.ds(start, tq), p * PAIR:(p + 1) * PAIR]
            acc_ref[p] += jnp.dot(w.astype(BF16), vb, preferred_element_type=F32)
            if ahead is not None:
                ahead[0][p] = scores(ahead[1], p)

    def softplus(z):
        return jnp.maximum(z, 0.0) + jnp.log(1.0 + jnp.exp2(jnp.abs(z) * (-LOG2_E)))

    def cumsum_dot(sp):
        hi, lo = _split_bf16(sp)
        return jnp.dot(jnp.concatenate([hi, lo], axis=1), csum, preferred_element_type=F32)

    def sweep_diagonal(ahead):
        start = pl.multiple_of(i * tq, tq)
        for p in range(n_pairs):
            z = scores(i, p)
            sp = jnp.where(strict, softplus(z), 0.0)
            carry = None
            la = [None] * n_sub
            for sb in reversed(range(n_sub)):
                cs = cumsum_dot(sp[:, sb * hb:(sb + 1) * hb])
                la[sb] = cs[:, :hb] if carry is None else cs[:, :hb] + carry
                carry = cs[:, hb:] if carry is None else carry + cs[:, hb:]
            carry_ref[p] = carry
            w = jnp.where(strict, jnp.exp(z + jnp.concatenate(la, axis=1)), 0.0)
            vb = v_ref[0, pl.ds(start, tq), p * PAIR:(p + 1) * PAIR]
            acc_ref[p] = jnp.dot(w.astype(BF16), vb, preferred_element_type=F32)
            ahead[0][p] = scores(ahead[1], p)

    @pl.when(i % 2 == 0)
    def _():
        sweep_diagonal(ahead=(za_ref, jnp.maximum(i - 1, 0)))

    @pl.when(i % 2 == 1)
    def _():
        sweep_diagonal(ahead=(zb_ref, i - 1))
        sweep(i - 1, zb_ref, ahead=(za_ref, jnp.maximum(i - 2, 0)))

    top = 2 * (i // 2) - 1

    def body(t, c):
        j = top - 2 * t
        sweep(j, za_ref, ahead=(zb_ref, j - 1))
        sweep(j - 1, zb_ref, ahead=(za_ref, jnp.maximum(j - 2, 0)))
        return c

    lax.fori_loop(0, i // 2, body, 0)
    for p in range(n_pairs):
        o_ref[0, :, p * PAIR:(p + 1) * PAIR] = jnp.where(
            first, acc_ref[p, 0:tq, :], acc_ref[p, tq:2 * tq, :]).astype(BF16)


def _attn_call(q, k, v):
    b, s, w = q.shape
    tq = ATT_TILE
    wl = ATT_PAIRS * PAIR if w % (ATT_PAIRS * PAIR) == 0 else PAIR
    assert s % tq == 0
    return pl.pallas_call(
        _attn_kernel,
        grid=(b, w // wl, s // tq),
        in_specs=[
            pl.BlockSpec((1, tq, wl), lambda bi, p, i: (bi, i, p)),
            pl.BlockSpec((1, s, wl), lambda bi, p, i: (bi, 0, p)),
            pl.BlockSpec((1, s, wl), lambda bi, p, i: (bi, 0, p)),
        ],
        out_specs=pl.BlockSpec((1, tq, wl), lambda bi, p, i: (bi, i, p)),
        out_shape=jax.ShapeDtypeStruct((b, s, w), BF16),
        scratch_shapes=[pltpu.VMEM((wl // PAIR, 2 * tq, PAIR), F32),
                        pltpu.VMEM((wl // PAIR, 2 * tq, SEQ_BLOCK), F32),
                        pltpu.VMEM((wl // PAIR, 2 * tq, tq), F32),
                        pltpu.VMEM((wl // PAIR, 2 * tq, tq), F32)],
        compiler_params=_cparams(("parallel", "parallel", "arbitrary")),
        name="sb_attn",
    )(q, k, v)


def _outproj_kernel(*refs, sbw, with_router):
    if with_router:
        oa_ref, ob_ref, x_ref, mod_ref, ng_ref, w_ref, rw_ref, rb_ref, x1_ref, h_ref, lg_ref = refs
    else:
        oa_ref, ob_ref, x_ref, mod_ref, ng_ref, w_ref, x1_ref, h_ref = refs
    y = jnp.dot(oa_ref[0], w_ref[0:sbw, :], preferred_element_type=F32)
    y = y + jnp.dot(ob_ref[0], w_ref[sbw:, :], preferred_element_type=F32)
    x1 = x_ref[0] + mod_ref[0, 0:1, :] * y
    x1_ref[0] = x1
    ms = jnp.mean(x1 * x1, axis=-1, keepdims=True)
    hn = x1 * lax.rsqrt(ms + EPS) * ng_ref[...]
    h = (hn * (1.0 + mod_ref[0, 2:3, :]) + mod_ref[0, 1:2, :]).astype(BF16)
    h_ref[0] = h
    if with_router:
        lg = lax.dot_general(rw_ref[...], h, (((1,), (1,)), ((), ())), preferred_element_type=F32)
        lg_ref[...] = lg[0:N_EXPERTS, :] + rb_ref[...]


def _outproj_call(oa, ob, x, mod, norm_g, w_out, router=None):
    b, s, d = x.shape
    sbw = oa.shape[-1]
    tm = min(TM_PROJ, s)
    row_spec = lambda w: pl.BlockSpec((1, tm, w), lambda i, j: (i, j, 0))
    const = lambda shape: pl.BlockSpec(shape, lambda i, j: (0,) * len(shape))
    in_specs = [row_spec(sbw), row_spec(ob.shape[-1]), row_spec(d),
                pl.BlockSpec((1, 3, d), lambda i, j: (i, 0, 0)), const((1, d)), const(w_out.shape)]
    out_specs = [row_spec(d), row_spec(d)]
    out_shape = [jax.ShapeDtypeStruct((b, s, d), F32), jax.ShapeDtypeStruct((b, s, d), BF16)]
    args = [oa, ob, x, mod, norm_g, w_out]
    if router is not None:
        rw_t, rb = router
        in_specs += [const(rw_t.shape), const(rb.shape)]
        nj = s // tm
        out_specs.append(pl.BlockSpec((N_EXPERTS, tm), lambda i, j: (0, i * nj + j)))
        out_shape.append(jax.ShapeDtypeStruct((N_EXPERTS, b * s), F32))
        args += [rw_t, rb]
    return pl.pallas_call(
        functools.partial(_outproj_kernel, sbw=sbw, with_router=router is not None),
        grid=(b, s // tm),
        in_specs=in_specs,
        out_specs=out_specs,
        out_shape=out_shape,
        compiler_params=_cparams(("parallel", "arbitrary")),
        name="mixer_out",
    )(*args)


def _swiglu_partial(h, wg_ref, wu_ref, wd_ref, parts):
    out = None
    for gcols, drows in parts:
        g = jnp.dot(h, wg_ref[gcols], preferred_element_type=F32)
        u = jnp.dot(h, wu_ref[gcols], preferred_element_type=F32)
        a = (g * jax.nn.sigmoid(g) * u).astype(BF16)
        y = jnp.dot(a, wd_ref[drows], preferred_element_type=F32)
        out = y if out is None else out + y
    return out


def _ffn_kernel(h_ref, x_ref, g2_ref, wg_ref, wu_ref, wd_ref, o_ref, acc_ref):
    h = h_ref[0]
    n_chunks = wg_ref.shape[0]
    for c0 in range(0, n_chunks, FFN_SUB):
        parts = [((c,), (c,)) for c in range(c0, min(c0 + FFN_SUB, n_chunks))]
        y = _swiglu_partial(h, wg_ref, wu_ref, wd_ref, parts)
        if c0 == 0:
            acc_ref[...] = y
        else:
            acc_ref[...] += y
    o_ref[0] = x_ref[0] + g2_ref[0] * acc_ref[...]


def _ffn_call(h, x, g2, wg, wu, wd):
    b, s, d = x.shape
    tm = min(TM_FFN, s)
    row_spec = pl.BlockSpec((1, tm, d), lambda i, j: (i, j, 0))
    resident = lambda a: pl.BlockSpec(a.shape, lambda i, j: (0, 0, 0), pipeline_mode=pl.Buffered(1))
    return pl.pallas_call(
        _ffn_kernel,
        grid=(b, s // tm),
        in_specs=[row_spec, row_spec, pl.BlockSpec((1, 1, d), lambda i, j: (i, 0, 0)),
                  resident(wg), resident(wu), resident(wd)],
        out_specs=row_spec,
        out_shape=jax.ShapeDtypeStruct((b, s, d), F32),
        scratch_shapes=[pltpu.VMEM((tm, d), F32)],
        compiler_params=_cparams(("parallel", "arbitrary")),
        name="dense_ffn",
    )(h, x, g2, wg, wu, wd)


def _route_kernel(lg_ref, oi_ref, of_ref, tile_ref, cnt_ref, carry_ref):
    step = pl.program_id(0)
    tl = lg_ref.shape[1]
    ne = N_EXPERTS

    @pl.when(step == 0)
    def _():
        carry_ref[...] = jnp.zeros_like(carry_ref)

    lg = lg_ref[...]
    eidx = lax.broadcasted_iota(jnp.int32, (ne, tl), 0).astype(F32)
    m1 = jnp.max(lg, axis=0, keepdims=True)
    i1 = jnp.min(jnp.where(lg == m1, eidx, float(ne)), axis=0, keepdims=True)
    lg2 = jnp.where(eidx == i1, -jnp.inf, lg)
    m2 = jnp.max(lg2, axis=0, keepdims=True)
    i2 = jnp.min(jnp.where(lg2 == m2, eidx, float(ne)), axis=0, keepdims=True)
    t = jnp.exp(m2 - m1)
    gate1 = 1.0 / (1.0 + t)
    gate2 = t / (1.0 + t)

    sel1 = eidx == i1
    sel2 = eidx == i2
    onehot = jnp.where(sel1 | sel2, 1.0, 0.0)
    a = lax.broadcasted_iota(jnp.int32, (tl, tl), 0)
    bcol = lax.broadcasted_iota(jnp.int32, (tl, tl), 1)
    upper = jnp.where(a < bcol, 1.0, 0.0).astype(BF16)
    lhs = jnp.concatenate([onehot, jnp.zeros_like(onehot)], axis=0).astype(BF16)
    excl_tile = jnp.dot(lhs, upper, preferred_element_type=F32)[0:ne, :]
    before = carry_ref[...]
    n_tile = jnp.sum(onehot, axis=1, keepdims=True) + jnp.zeros_like(before)
    n_tile = jnp.floor((n_tile + (SUBLANES - 1)) * (1.0 / SUBLANES)) * SUBLANES
    off = jnp.concatenate([jnp.zeros((1, LANES), F32)] +
                          [jnp.sum(n_tile[0:e], axis=0, keepdims=True) for e in range(1, ne)], axis=0)
    pick = lambda sel, v: jnp.sum(jnp.where(sel, v, 0.0), axis=0, keepdims=True)
    r1 = pick(sel1, excl_tile + before[:, 0:1])
    r2 = pick(sel2, excl_tile + before[:, 0:1])
    p1 = pick(sel1, excl_tile + off[:, 0:1])
    p2 = pick(sel2, excl_tile + off[:, 0:1])
    carry_ref[...] = before + n_tile

    as_i = lambda v: v.astype(jnp.int32)
    oi_ref[...] = jnp.concatenate(
        [as_i(i1), as_i(i2), as_i(r1), as_i(r2), as_i(p1), as_i(p2), jnp.zeros((ne - 6, tl), jnp.int32)],
        axis=0)
    of_ref[...] = jnp.concatenate([gate1, gate2, jnp.zeros((ne - 2, tl), F32)], axis=0)
    lane = lax.broadcasted_iota(jnp.int32, (ne, LANES), 1)
    tile_ref[0] = as_i(jnp.where(lane == 0, n_tile, jnp.where(lane == 1, before, 0.0)))
    cnt_ref[...] = carry_ref[...]


def _route_call(logits_t):
    ne, t = logits_t.shape
    tl = min(TT_ROWS, t)
    blk = pl.BlockSpec((ne, tl), lambda i: (0, i))
    return pl.pallas_call(
        _route_kernel,
        grid=(t // tl,),
        in_specs=[blk],
        out_specs=[blk, blk, pl.BlockSpec((1, ne, LANES), lambda i: (i, 0, 0)),
                   pl.BlockSpec((ne, LANES), lambda i: (0, 0))],
        out_shape=[jax.ShapeDtypeStruct((ne, t), jnp.int32), jax.ShapeDtypeStruct((ne, t), F32),
                   jax.ShapeDtypeStruct((t // tl, ne, LANES), jnp.int32),
                   jax.ShapeDtypeStruct((ne, LANES), F32)],
        scratch_shapes=[pltpu.VMEM((ne, LANES), F32)],
        compiler_params=_cparams(("arbitrary",)),
        name="moe_route",
    )(logits_t)


def _run_pieces(n, max_rows):
    size = max_rows
    while size >= SUBLANES:
        yield size, pl.multiple_of(n & ~(2 * size - 1), SUBLANES), (n & size) != 0
        size //= 2


def _compact_rows(tt):
    return 2 * tt + N_EXPERTS * SUBLANES


def _dispatch_kernel(tail_ref, cnt_ref, base_ref, pos_ref, h_ref, xb_ref, c_ref, zero_ref, sem, zsem):
    step = pl.program_id(0)
    tt = h_ref.shape[0]
    tmb = zero_ref.shape[0]

    def zero_block(blk):
        row = pl.multiple_of(blk * tmb, tmb)
        return pltpu.make_async_copy(zero_ref, xb_ref.at[pl.ds(row, tmb)], zsem)

    @pl.when(step == 0)
    def _():
        zero_ref[...] = jnp.zeros_like(zero_ref)
        for e in range(N_EXPERTS):
            zero_block(tail_ref[e]).start()
        for e in range(N_EXPERTS):
            zero_block(tail_ref[e]).wait()

        def zero_unused(blk, c):
            cp = zero_block(blk)
            cp.start()
            cp.wait()
            return c

        lax.fori_loop(tail_ref[N_EXPERTS], xb_ref.shape[0] // tmb, zero_unused, 0)

    slot = step % 2
    r = lax.broadcasted_iota(jnp.int32, (c_ref.shape[1], tt), 0)
    onehot = jnp.where((r == pos_ref[4:5, :]) | (r == pos_ref[5:6, :]), 1.0, 0.0).astype(BF16)
    c_ref[slot] = jnp.dot(onehot, h_ref[...], preferred_element_type=F32)

    def run_copies(op, tile, buf):
        off = 0
        for e in range(N_EXPERTS):
            n = cnt_ref[tile * N_EXPERTS + e]
            dst = pl.multiple_of(base_ref[tile * N_EXPERTS + e], SUBLANES)
            for size, piece, present in _run_pieces(n, tt):
                @pl.when(present)
                def _():
                    src = pl.multiple_of(off + piece, SUBLANES)
                    op(pltpu.make_async_copy(c_ref.at[buf, pl.ds(src, size)],
                                             xb_ref.at[pl.ds(pl.multiple_of(dst + piece, SUBLANES), size)],
                                             sem.at[buf]))
            off = off + n

    run_copies(lambda cp: cp.start(), step, slot)

    @pl.when(step > 0)
    def _():
        run_copies(lambda cp: cp.wait(), step - 1, 1 - slot)

    @pl.when(step == pl.num_programs(0) - 1)
    def _():
        run_copies(lambda cp: cp.wait(), step, slot)


def _dispatch_call(tail_blocks, tile_cnt, tile_base, pos, h, n_slots):
    t, d = h.shape
    tt = min(TT_ROWS, t)
    tmb = TM_MOE
    return pl.pallas_call(
        _dispatch_kernel,
        grid_spec=pltpu.PrefetchScalarGridSpec(
            num_scalar_prefetch=3,
            grid=(t // tt,),
            in_specs=[pl.BlockSpec((N_EXPERTS, tt), lambda i, *_: (0, i)),
                      pl.BlockSpec((tt, d), lambda i, *_: (i, 0))],
            out_specs=pl.BlockSpec(memory_space=pl.ANY),
            scratch_shapes=[pltpu.VMEM((2, _compact_rows(tt), d), F32), pltpu.VMEM((tmb, d), F32),
                            pltpu.SemaphoreType.DMA((2,)), pltpu.SemaphoreType.DMA],
        ),
        out_shape=jax.ShapeDtypeStruct((n_slots, d), F32),
        compiler_params=_cparams(("arbitrary",)),
        name="moe_dispatch",
    )(tail_blocks, tile_cnt, tile_base, pos, h)


def _moe_ffn_kernel(be_ref, nb_ref, x_ref, wg_ref, wu_ref, wd_ref, o_ref, xb_ref):
    i = pl.program_id(0)
    c = pl.program_id(1)
    used = i < nb_ref[0]
    tf = wg_ref.shape[2]
    wpart = min(tf, MXU_DIM)
    parts = [((0, slice(None), slice(p, p + wpart)), (0, slice(p, p + wpart), slice(None)))
             for p in range(0, tf, wpart)]

    @pl.when(used & (c == 0))
    def _():
        xb_ref[...] = x_ref[...].astype(BF16)
        o_ref[...] = _swiglu_partial(xb_ref[...], wg_ref, wu_ref, wd_ref, parts)

    @pl.when(used & (c != 0))
    def _():
        o_ref[...] += _swiglu_partial(xb_ref[...], wg_ref, wu_ref, wd_ref, parts)

    @pl.when(jnp.logical_not(used) & (c == 0))
    def _():
        o_ref[...] = jnp.zeros_like(o_ref)


def _moe_ffn_call(block_e, n_used, xb, wg, wu, wd):
    n_slots, d = xb.shape
    tmb = TM_MOE
    nb = n_slots // tmb
    f = wg.shape[2]
    tf = _largest_divisor(f, (TF_MOE, 512, 256, 128))
    nc = f // tf
    blk = lambda i, nbu: jnp.minimum(i, nbu[0] - 1)
    chunk = lambda i, c, nbu: jnp.where(i < nbu[0], c, nc - 1)
    return pl.pallas_call(
        _moe_ffn_kernel,
        grid_spec=pltpu.PrefetchScalarGridSpec(
            num_scalar_prefetch=2,
            grid=(nb, nc),
            in_specs=[
                pl.BlockSpec((tmb, d), lambda i, c, be, nbu: (blk(i, nbu), 0)),
                pl.BlockSpec((1, d, tf), lambda i, c, be, nbu: (be[blk(i, nbu)], 0, chunk(i, c, nbu))),
                pl.BlockSpec((1, d, tf), lambda i, c, be, nbu: (be[blk(i, nbu)], 0, chunk(i, c, nbu))),
                pl.BlockSpec((1, tf, d), lambda i, c, be, nbu: (be[blk(i, nbu)], chunk(i, c, nbu), 0)),
            ],
            out_specs=pl.BlockSpec((tmb, d), lambda i, c, be, nbu: (i, 0)),
            scratch_shapes=[pltpu.VMEM((tmb, d), BF16)],
        ),
        out_shape=jax.ShapeDtypeStruct((n_slots, d), F32),
        compiler_params=_cparams(("arbitrary", "arbitrary")),
        name="moe_ffn",
    )(block_e, n_used, xb, wg, wu, wd)


def _combine_kernel(dest_ref, yb_ref, x_ref, g2_ref, gate_ref, o_ref, rows_ref, sem):
    tt = x_ref.shape[1]

    def row_copy(t, k):
        return pltpu.make_async_copy(yb_ref.at[pl.ds(dest_ref[0, k, t], 1)],
                                     rows_ref.at[k, pl.ds(t, 1)], sem)

    def wait(t, c):
        row_copy(t, 0).wait()
        row_copy(t, 1).wait()
        return c

    for t in range(tt):
        row_copy(t, 0).start(priority=0)
        row_copy(t, 1).start(priority=1)
    lax.fori_loop(0, tt, wait, 0, unroll=8)
    y = rows_ref[0] * gate_ref[0, :, 0:1] + rows_ref[1] * gate_ref[0, :, 1:2]
    o_ref[0] = x_ref[0] + g2_ref[0] * y


def _combine_call(dest, yb, x, g2, gates):
    b, s, d = x.shape
    tt = dest.shape[2]
    nj = s // tt
    row_spec = pl.BlockSpec((1, tt, d), lambda i, j: (i, j, 0))
    return pl.pallas_call(
        _combine_kernel,
        grid=(b, nj),
        in_specs=[
            pl.BlockSpec((1, 2, tt), lambda i, j: (i * nj + j, 0, 0), memory_space=pltpu.SMEM),
            pl.BlockSpec(memory_space=pl.ANY),
            row_spec,
            pl.BlockSpec((1, 1, d), lambda i, j: (i, 0, 0)),
            pl.BlockSpec((1, tt, 2), lambda i, j: (i * nj + j, 0, 0)),
        ],
        out_specs=row_spec,
        out_shape=jax.ShapeDtypeStruct((b, s, d), F32),
        scratch_shapes=[pltpu.VMEM((2, tt, d), F32), pltpu.SemaphoreType.DMA],
        compiler_params=_cparams(("arbitrary", "arbitrary")),
        name="moe_combine",
    )(dest, yb, x, g2, gates)


def _chunk_cols(w, tf):
    d, f = w.shape
    return w.reshape(d, f // tf, tf).transpose(1, 0, 2).astype(BF16)


def _moe_layer(h, x1, g2, logits_t, wg, wu, wd):
    b, s, d = x1.shape
    t = b * s
    tmb = TM_MOE
    tt = min(TT_ROWS, s)
    ne = N_EXPERTS
    oi, of, tiles, cnt = _route_call(logits_t)
    e1, e2, r1, r2 = oi[0], oi[1], oi[2], oi[3]
    counts = cnt[:, 0].astype(jnp.int32)
    padded = (counts + tmb - 1) // tmb * tmb
    pad_end = jnp.cumsum(padded)
    pad_start = pad_end - padded
    n_used = (pad_end[-1] // tmb).astype(jnp.int32)
    nb = -(-(2 * t + (t // tt) * ne * (SUBLANES - 1)) // tmb) + ne
    block_row0 = jnp.arange(nb, dtype=jnp.int32) * tmb
    block_e = jnp.minimum(jnp.sum(pad_end[None, :] <= block_row0[:, None], axis=1), ne - 1).astype(jnp.int32)
    tail_blocks = jnp.where(padded > 0, pad_end // tmb - 1, nb - 1)
    tail_blocks = jnp.concatenate([tail_blocks, n_used[None]]).astype(jnp.int32)
    start_of = lambda e: jnp.sum(jnp.where(e[None, :] == jnp.arange(ne)[:, None], pad_start[:, None], 0), axis=0)
    d1 = start_of(e1) + r1
    d2 = start_of(e2) + r2
    dest = jnp.stack([d1.reshape(t // tt, tt), d2.reshape(t // tt, tt)], axis=1).astype(jnp.int32)

    tile_cnt = tiles[:, :, 0].reshape(-1)
    tile_base = (tiles[:, :, 1] + pad_start[None, :]).reshape(-1).astype(jnp.int32)
    xb = _dispatch_call(tail_blocks, tile_cnt, tile_base, oi, h.reshape(t, d), nb * tmb)
    yb = _moe_ffn_call(block_e, n_used.reshape(1), xb, wg, wu, wd)
    gates = jnp.stack([of[0], of[1]], axis=-1).reshape(t // tt, tt, 2)
    return _combine_call(dest, yb, x1, g2, gates)


def kernel(x, c, ada_w, ada_b, norm_mix_g, norm_ffn_g, w_in, q_norm_g, k_norm_g, sg_ln_g, sg_ln_b,
           sg_w_spatial, sg_b_spatial, w_out, ffn_w_gate, ffn_w_up, ffn_w_down, router_w, router_b,
           moe_w_gate, moe_w_up, moe_w_down):
    b, s, d = x.shape
    depth = ada_w.shape[0]
    sgw = sg_ln_g.shape[-1]
    sbw = w_out.shape[1] - sgw
    assert sbw % PAIR == 0 and sgw % PAIR == 0 and s % SEQ_BLOCK == 0
    assert w_in.shape[-1] == 3 * sbw + 2 * sgw

    ada = _ada_call(c, ada_w, ada_b)
    ada = ada.reshape(depth, b, 6, d)

    wb = MXU_DIM if sbw % MXU_DIM == 0 else PAIR
    gi = jnp.arange(wb) // HEAD_DIM
    bd = (gi[:, None] == gi[None, :]).astype(BF16)
    pos = jnp.arange(SEQ_BLOCK)
    chunk_causal = (pos[None, :] // CHUNK) <= (pos[:, None] // CHUNK)
    scale = HEAD_DIM ** -0.5

    for l in range(depth):
        sh1, sc1, g1, sh2, sc2, g2 = (ada[l, :, j] for j in range(6))
        qk_g = jnp.stack([jnp.tile(q_norm_g[l], sbw // HEAD_DIM) * scale,
                          jnp.tile(k_norm_g[l], sbw // HEAD_DIM)])
        ln = jnp.stack([sg_ln_g[l], sg_ln_b[l]])
        ws = jnp.where(chunk_causal[None], sg_w_spatial[l], 0.0)
        ws = ws.reshape(sgw // PAIR, 2, SEQ_BLOCK, SEQ_BLOCK).transpose(0, 2, 1, 3)
        ws = ws.reshape(sgw // PAIR, SEQ_BLOCK, 2 * SEQ_BLOCK).astype(BF16)
        bias = jnp.repeat(sg_b_spatial[l].T, HEAD_DIM, axis=1)

        q, k, v, ob = _inproj_call(x, jnp.stack([sh1, sc1], axis=1), norm_mix_g[l][None],
                                   w_in[l].astype(BF16), qk_g, bd, ln, ws, bias, sbw=sbw, sgw=sgw)
        oa = _attn_call(q, k, v)
        mod2 = jnp.stack([g1, sh2, sc2], axis=1)
        j = l // 2
        if l % 2 == 0:
            x1, h = _outproj_call(oa, ob, x, mod2, norm_ffn_g[l][None], w_out[l].astype(BF16))
            tf = _largest_divisor(ffn_w_gate.shape[-1], (512, 256, 128))
            wg = _chunk_cols(ffn_w_gate[j], tf)
            wu = _chunk_cols(ffn_w_up[j], tf)
            wd = ffn_w_down[j].reshape(-1, tf, d).astype(BF16)
            x = _ffn_call(h, x1, g2[:, None], wg, wu, wd)
        else:
            rw_t = jnp.zeros((2 * N_EXPERTS, d), BF16).at[:N_EXPERTS].set(router_w[j].T.astype(BF16))
            rb = router_b[j].reshape(N_EXPERTS, 1)
            x1, h, logits_t = _outproj_call(oa, ob, x, mod2, norm_ffn_g[l][None],
                                            w_out[l].astype(BF16), router=(rw_t, rb))
            x = _moe_layer(h, x1, g2[:, None], logits_t, moe_w_gate[j].astype(BF16),
                           moe_w_up[j].astype(BF16), moe_w_down[j].astype(BF16))
    return x
```

```python
import functools
import math

import jax
import jax.numpy as jnp
from jax import lax
from jax.experimental import pallas as pl
from jax.experimental.pallas import tpu as pltpu

F32 = jnp.float32
BF16 = jnp.bfloat16

HEAD_DIM = 64
PAIR = 2 * HEAD_DIM
SEQ_BLOCK = 128
CHUNK = 64
N_EXPERTS = 8
EPS = 1e-6
LOG2_E = 1.4426950408889634

LANES = 128
SUBLANES = 8
MXU_DIM = 256
VMEM_LIMIT_BYTES = 56 * 1024 * 1024

ATT_TILE = MXU_DIM
ATT_PAIRS = 4
TM_PROJ = 1024
TM_FFN = 512
TM_MOE = 1024
TF_MOE = 1792
FFN_SUB = 2
TT_ROWS = 512


def _cparams(sem):
    return pltpu.CompilerParams(dimension_semantics=sem, vmem_limit_bytes=VMEM_LIMIT_BYTES)


def _split_bf16(x):
    hi = x.astype(BF16)
    lo = (x - hi.astype(F32)).astype(BF16)
    return hi, lo


def _largest_divisor(n, candidates):
    for c in candidates:
        if n % c == 0:
            return c
    raise ValueError(f"no tile in {candidates} divides {n}")


def _ada_kernel(c_ref, w_ref, b_ref, o_ref):
    c = c_ref[...]
    c_act = (c * jax.nn.sigmoid(c)).astype(BF16)
    o_ref[0] = jnp.dot(c_act, w_ref[0].astype(BF16), preferred_element_type=F32) + b_ref[0]


def _ada_call(c, ada_w, ada_b):
    depth, d, n = ada_w.shape
    b = c.shape[0]
    tn = _largest_divisor(n, (1536, 1024, 512, 256, 128))
    return pl.pallas_call(
        _ada_kernel,
        grid=(depth, n // tn),
        in_specs=[
            pl.BlockSpec((b, d), lambda l, j: (0, 0)),
            pl.BlockSpec((1, d, tn), lambda l, j: (l, 0, j)),
            pl.BlockSpec((1, 1, tn), lambda l, j: (l, 0, j)),
        ],
        out_specs=pl.BlockSpec((1, b, tn), lambda l, j: (l, 0, j)),
        out_shape=jax.ShapeDtypeStruct((depth, b, n), F32),
        compiler_params=_cparams(("arbitrary", "arbitrary")),
        name="ada_ln",
    )(c, ada_w, ada_b.reshape(depth, 1, n))


def _group_mean_sq(t, bd):
    wb = bd.shape[0]
    sq = t * t
    hi, lo = _split_bf16(sq)
    cols = []
    for c0 in range(0, t.shape[1], wb):
        s = jnp.dot(hi[:, c0:c0 + wb], bd, preferred_element_type=F32)
        s = s + jnp.dot(lo[:, c0:c0 + wb], bd, preferred_element_type=F32)
        cols.append(s)
    ss = cols[0] if len(cols) == 1 else jnp.concatenate(cols, axis=1)
    return ss * (1.0 / HEAD_DIM)


def _gelu(t):
    return 0.5 * t * (1.0 + lax.erf(t * math.sqrt(0.5)))


def _inproj_kernel(x_ref, mod_ref, ng_ref, w_ref, qkg_ref, bd_ref, ln_ref, ws_ref, bias_ref,
                   q_ref, k_ref, v_ref, ob_ref, *, sbw, sgw):
    xf = x_ref[0]
    ms = jnp.mean(xf * xf, axis=-1, keepdims=True)
    y = xf * lax.rsqrt(ms + EPS) * ng_ref[...]
    h = (y * (1.0 + mod_ref[0, 1:2, :]) + mod_ref[0, 0:1, :]).astype(BF16)

    g = _gelu(jnp.dot(h, w_ref[:, 3 * sbw + sgw:3 * sbw + 2 * sgw], preferred_element_type=F32))
    u = _gelu(jnp.dot(h, w_ref[:, 3 * sbw:3 * sbw + sgw], preferred_element_type=F32))
    mu = jnp.mean(g, axis=-1, keepdims=True)
    gc = g - mu
    var = jnp.mean(gc * gc, axis=-1, keepdims=True)
    gn = (gc * lax.rsqrt(var + EPS) * ln_ref[0:1, :] + ln_ref[1:2, :]).astype(BF16)

    bd = bd_ref[...]
    q = jnp.dot(h, w_ref[:, 0:sbw], preferred_element_type=F32)
    qn = q * lax.rsqrt(_group_mean_sq(q, bd) + EPS) * qkg_ref[0:1, :]
    q_ref[0] = qn.astype(BF16)
    k = jnp.dot(h, w_ref[:, sbw:2 * sbw], preferred_element_type=F32)
    kn = k * lax.rsqrt(_group_mean_sq(k, bd) + EPS) * qkg_ref[1:2, :]
    k_ref[0] = kn.astype(BF16)
    v_ref[0] = jnp.dot(h, w_ref[:, 2 * sbw:3 * sbw], preferred_element_type=F32).astype(BF16)

    first = lax.broadcasted_iota(jnp.int32, (SEQ_BLOCK, PAIR), 1) < HEAD_DIM
    zero = jnp.zeros((SEQ_BLOCK, PAIR), BF16)
    n_blocks = xf.shape[0] // SEQ_BLOCK
    per_dot = MXU_DIM // PAIR if n_blocks % (MXU_DIM // PAIR) == 0 else 1
    for r0 in range(0, n_blocks, per_dot):
        blocks = [slice((r0 + n) * SEQ_BLOCK, (r0 + n + 1) * SEQ_BLOCK) for n in range(per_dot)]
        for p in range(sgw // PAIR):
            cols = slice(p * PAIR, (p + 1) * PAIR)
            vbs = [gn[rows, cols] for rows in blocks]
            top = jnp.concatenate([jnp.where(first, vb, zero) for vb in vbs], axis=1)
            bot = jnp.concatenate([jnp.where(first, zero, vb) for vb in vbs], axis=1)
            mixed = jnp.dot(ws_ref[p], jnp.concatenate([top, bot], axis=0), preferred_element_type=F32)
            for n, rows in enumerate(blocks):
                m = mixed[:, n * PAIR:(n + 1) * PAIR]
                ob_ref[0, rows, cols] = (u[rows, cols] * (m + bias_ref[:, cols])).astype(BF16)


def _inproj_call(x, mod, norm_g, w_in, qk_g, bd, ln, ws, bias, *, sbw, sgw):
    b, s, d = x.shape
    tm = min(TM_PROJ, s)
    n_in = w_in.shape[1]
    out = jax.ShapeDtypeStruct((b, s, sbw), BF16)
    row_spec = lambda w: pl.BlockSpec((1, tm, w), lambda i, j: (i, j, 0))
    const = lambda shape: pl.BlockSpec(shape, lambda i, j: (0,) * len(shape))
    return pl.pallas_call(
        functools.partial(_inproj_kernel, sbw=sbw, sgw=sgw),
        grid=(b, s // tm),
        in_specs=[
            row_spec(d),
            pl.BlockSpec((1, 2, d), lambda i, j: (i, 0, 0)),
            const((1, d)),
            const((d, n_in)),
            const((2, sbw)),
            const(bd.shape),
            const((2, sgw)),
            const(ws.shape),
            const((SEQ_BLOCK, sgw)),
        ],
        out_specs=[row_spec(sbw), row_spec(sbw), row_spec(sbw), row_spec(sgw)],
        out_shape=[out, out, out, jax.ShapeDtypeStruct((b, s, sgw), BF16)],
        compiler_params=_cparams(("parallel", "arbitrary")),
        name="mixer_in",
    )(x, mod, norm_g, w_in, qk_g, bd, ln, ws, bias)


def _attn_kernel(q_ref, k_ref, v_ref, o_ref, acc_ref, carry_ref, za_ref, zb_ref):
    i = pl.program_id(2)
    tq = ATT_TILE
    hb = SEQ_BLOCK
    n_sub = tq // hb
    n_pairs = q_ref.shape[2] // PAIR
    first = lax.broadcasted_iota(jnp.int32, (tq, PAIR), 1) < HEAD_DIM
    zero = jnp.zeros((tq, PAIR), BF16)
    q2 = []
    for p in range(n_pairs):
        q = q_ref[0, :, p * PAIR:(p + 1) * PAIR]
        q2.append(jnp.concatenate([jnp.where(first, q, zero), jnp.where(first, zero, q)], axis=0))

    kk = lax.broadcasted_iota(jnp.int32, (2 * hb, 2 * hb), 0) & (hb - 1)
    nn = lax.broadcasted_iota(jnp.int32, (2 * hb, 2 * hb), 1)
    csum = jnp.where((kk >= nn) | (nn >= hb), -1.0, 0.0).astype(BF16)

    qpos = lax.broadcasted_iota(jnp.int32, (2 * tq, tq), 0) & (tq - 1)
    kpos = lax.broadcasted_iota(jnp.int32, (2 * tq, tq), 1)
    strict = kpos < qpos

    def scores(j, p):
        start = pl.multiple_of(j * tq, tq)
        kb = k_ref[0, pl.ds(start, tq), p * PAIR:(p + 1) * PAIR]
        return lax.dot_general(q2[p], kb, (((1,), (1,)), ((), ())), preferred_element_type=F32)

    def sweep(j, z_ref, ahead=None):
        start = pl.multiple_of(j * tq, tq)
        for p in range(n_pairs):
            z = z_ref[p]
            sp = softplus(z)
            carry = carry_ref[p]
            la = [None] * n_sub
            for sb in reversed(range(n_sub)):
                cs = cumsum_dot(sp[:, sb * hb:(sb + 1) * hb])
                la[sb] = cs[:, :hb] + carry
                carry = carry + cs[:, hb:]
            carry_ref[p] = carry
            w = jnp.exp(z + jnp.concatenate(la, axis=1))
            vb = v_ref[0, pl.ds(start, tq), p * PAIR:(p + 1) * PAIR]
            acc_ref[p] += jnp.dot(w.astype(BF16), vb, preferred_element_type=F32)
            if ahead is not None:
                ahead[0][p] = scores(ahead[1], p)

    def softplus(z):
        return jnp.maximum(z, 0.0) + jnp.log(1.0 + jnp.exp2(jnp.abs(z) * (-LOG2_E)))

    def cumsum_dot(sp):
        hi, lo = _split_bf16(sp)
        return jnp.dot(jnp.concatenate([hi, lo], axis=1), csum, preferred_element_type=F32)

    def sweep_diagonal(ahead):
        start = pl.multiple_of(i * tq, tq)
        z_next = scores(i, 0)
        for p in range(n_pairs):
            z = z_next
            if p + 1 < n_pairs:
                z_next = scores(i, p + 1)
            sp = jnp.where(strict, softplus(z), 0.0)
            carry = None
            la = [None] * n_sub
            for sb in reversed(range(n_sub)):
                cs = cumsum_dot(sp[:, sb * hb:(sb + 1) * hb])
                la[sb] = cs[:, :hb] if carry is None else cs[:, :hb] + carry
                carry = cs[:, hb:] if carry is None else carry + cs[:, hb:]
            carry_ref[p] = carry
            w = jnp.where(strict, jnp.exp(z + jnp.concatenate(la, axis=1)), 0.0)
            vb = v_ref[0, pl.ds(start, tq), p * PAIR:(p + 1) * PAIR]
            acc_ref[p] = jnp.dot(w.astype(BF16), vb, preferred_element_type=F32)
            ahead[0][p] = scores(ahead[1], p)

    @pl.when(i % 2 == 0)
    def _():
        sweep_diagonal(ahead=(za_ref, jnp.maximum(i - 1, 0)))

    @pl.when(i % 2 == 1)
    def _():
        sweep_diagonal(ahead=(zb_ref, i - 1))
        sweep(i - 1, zb_ref, ahead=(za_ref, jnp.maximum(i - 2, 0)))

    top = 2 * (i // 2) - 1

    def body(t, c):
        j = top - 2 * t
        sweep(j, za_ref, ahead=(zb_ref, j - 1))
        sweep(j - 1, zb_ref, ahead=(za_ref, jnp.maximum(j - 2, 0)))
        return c

    lax.fori_loop(0, i // 2, body, 0)
    for p in range(n_pairs):
        o_ref[0, :, p * PAIR:(p + 1) * PAIR] = jnp.where(
            first, acc_ref[p, 0:tq, :], acc_ref[p, tq:2 * tq, :]).astype(BF16)


def _attn_call(q, k, v):
    b, s, w = q.shape
    tq = ATT_TILE
    wl = ATT_PAIRS * PAIR if w % (ATT_PAIRS * PAIR) == 0 else PAIR
    assert s % tq == 0
    return pl.pallas_call(
        _attn_kernel,
        grid=(b, w // wl, s // tq),
        in_specs=[
            pl.BlockSpec((1, tq, wl), lambda bi, p, i: (bi, i, p)),
            pl.BlockSpec((1, s, wl), lambda bi, p, i: (bi, 0, p)),
            pl.BlockSpec((1, s, wl), lambda bi, p, i: (bi, 0, p)),
        ],
        out_specs=pl.BlockSpec((1, tq, wl), lambda bi, p, i: (bi, i, p)),
        out_shape=jax.ShapeDtypeStruct((b, s, w), BF16),
        scratch_shapes=[pltpu.VMEM((wl // PAIR, 2 * tq, PAIR), F32),
                        pltpu.VMEM((wl // PAIR, 2 * tq, SEQ_BLOCK), F32),
                        pltpu.VMEM((wl // PAIR, 2 * tq, tq), F32),
                        pltpu.VMEM((wl // PAIR, 2 * tq, tq), F32)],
        compiler_params=_cparams(("parallel", "parallel", "arbitrary")),
        name="sb_attn",
    )(q, k, v)


def _outproj_kernel(*refs, sbw, with_router):
    if with_router:
        oa_ref, ob_ref, x_ref, mod_ref, ng_ref, w_ref, rw_ref, rb_ref, x1_ref, h_ref, lg_ref = refs
    else:
        oa_ref, ob_ref, x_ref, mod_ref, ng_ref, w_ref, x1_ref, h_ref = refs
    y = jnp.dot(oa_ref[0], w_ref[0:sbw, :], preferred_element_type=F32)
    y = y + jnp.dot(ob_ref[0], w_ref[sbw:, :], preferred_element_type=F32)
    x1 = x_ref[0] + mod_ref[0, 0:1, :] * y
    x1_ref[0] = x1
    ms = jnp.mean(x1 * x1, axis=-1, keepdims=True)
    hn = x1 * lax.rsqrt(ms + EPS) * ng_ref[...]
    h = (hn * (1.0 + mod_ref[0, 2:3, :]) + mod_ref[0, 1:2, :]).astype(BF16)
    h_ref[0] = h
    if with_router:
        lg = lax.dot_general(rw_ref[...], h, (((1,), (1,)), ((), ())), preferred_element_type=F32)
        lg_ref[...] = lg[0:N_EXPERTS, :] + rb_ref[...]


def _outproj_call(oa, ob, x, mod, norm_g, w_out, router=None):
    b, s, d = x.shape
    sbw = oa.shape[-1]
    tm = min(TM_PROJ, s)
    row_spec = lambda w: pl.BlockSpec((1, tm, w), lambda i, j: (i, j, 0))
    const = lambda shape: pl.BlockSpec(shape, lambda i, j: (0,) * len(shape))
    in_specs = [row_spec(sbw), row_spec(ob.shape[-1]), row_spec(d),
                pl.BlockSpec((1, 3, d), lambda i, j: (i, 0, 0)), const((1, d)), const(w_out.shape)]
    out_specs = [row_spec(d), row_spec(d)]
    out_shape = [jax.ShapeDtypeStruct((b, s, d), F32), jax.ShapeDtypeStruct((b, s, d), BF16)]
    args = [oa, ob, x, mod, norm_g, w_out]
    if router is not None:
        rw_t, rb = router
        in_specs += [const(rw_t.shape), const(rb.shape)]
        nj = s // tm
        out_specs.append(pl.BlockSpec((N_EXPERTS, tm), lambda i, j: (0, i * nj + j)))
        out_shape.append(jax.ShapeDtypeStruct((N_EXPERTS, b * s), F32))
        args += [rw_t, rb]
    return pl.pallas_call(
        functools.partial(_outproj_kernel, sbw=sbw, with_router=router is not None),
        grid=(b, s // tm),
        in_specs=in_specs,
        out_specs=out_specs,
        out_shape=out_shape,
        compiler_params=_cparams(("parallel", "arbitrary")),
        name="mixer_out",
    )(*args)


def _swiglu_partial(h, wg_ref, wu_ref, wd_ref, parts):
    out = None
    for gcols, drows in parts:
        g = jnp.dot(h, wg_ref[gcols], preferred_element_type=F32)
        u = jnp.dot(h, wu_ref[gcols], preferred_element_type=F32)
        a = (g * jax.nn.sigmoid(g) * u).astype(BF16)
        y = jnp.dot(a, wd_ref[drows], preferred_element_type=F32)
        out = y if out is None else out + y
    return out


def _ffn_kernel(h_ref, x_ref, g2_ref, wg_ref, wu_ref, wd_ref, o_ref, acc_ref):
    h = h_ref[0]
    n_chunks = wg_ref.shape[0]
    for c0 in range(0, n_chunks, FFN_SUB):
        parts = [((c,), (c,)) for c in range(c0, min(c0 + FFN_SUB, n_chunks))]
        y = _swiglu_partial(h, wg_ref, wu_ref, wd_ref, parts)
        if c0 == 0:
            acc_ref[...] = y
        else:
            acc_ref[...] += y
    o_ref[0] = x_ref[0] + g2_ref[0] * acc_ref[...]


def _ffn_call(h, x, g2, wg, wu, wd):
    b, s, d = x.shape
    tm = min(TM_FFN, s)
    row_spec = pl.BlockSpec((1, tm, d), lambda i, j: (i, j, 0))
    resident = lambda a: pl.BlockSpec(a.shape, lambda i, j: (0, 0, 0), pipeline_mode=pl.Buffered(1))
    return pl.pallas_call(
        _ffn_kernel,
        grid=(b, s // tm),
        in_specs=[row_spec, row_spec, pl.BlockSpec((1, 1, d), lambda i, j: (i, 0, 0)),
                  resident(wg), resident(wu), resident(wd)],
        out_specs=row_spec,
        out_shape=jax.ShapeDtypeStruct((b, s, d), F32),
        scratch_shapes=[pltpu.VMEM((tm, d), F32)],
        compiler_params=_cparams(("parallel", "arbitrary")),
        name="dense_ffn",
    )(h, x, g2, wg, wu, wd)


def _route_kernel(lg_ref, oi_ref, of_ref, tile_ref, cnt_ref, carry_ref):
    step = pl.program_id(0)
    tl = lg_ref.shape[1]
    ne = N_EXPERTS

    @pl.when(step == 0)
    def _():
        carry_ref[...] = jnp.zeros_like(carry_ref)

    lg = lg_ref[...]
    eidx = lax.broadcasted_iota(jnp.int32, (ne, tl), 0).astype(F32)
    m1 = jnp.max(lg, axis=0, keepdims=True)
    i1 = jnp.min(jnp.where(lg == m1, eidx, float(ne)), axis=0, keepdims=True)
    lg2 = jnp.where(eidx == i1, -jnp.inf, lg)
    m2 = jnp.max(lg2, axis=0, keepdims=True)
    i2 = jnp.min(jnp.where(lg2 == m2, eidx, float(ne)), axis=0, keepdims=True)
    t = jnp.exp(m2 - m1)
    gate1 = 1.0 / (1.0 + t)
    gate2 = t / (1.0 + t)

    sel1 = eidx == i1
    sel2 = eidx == i2
    onehot = jnp.where(sel1 | sel2, 1.0, 0.0)
    a = lax.broadcasted_iota(jnp.int32, (tl, tl), 0)
    bcol = lax.broadcasted_iota(jnp.int32, (tl, tl), 1)
    upper = jnp.where(a < bcol, 1.0, 0.0).astype(BF16)
    lhs = jnp.concatenate([onehot, jnp.zeros_like(onehot)], axis=0).astype(BF16)
    excl_tile = jnp.dot(lhs, upper, preferred_element_type=F32)[0:ne, :]
    before = carry_ref[...]
    n_tile = jnp.sum(onehot, axis=1, keepdims=True) + jnp.zeros_like(before)
    n_tile = jnp.floor((n_tile + (SUBLANES - 1)) * (1.0 / SUBLANES)) * SUBLANES
    off = jnp.concatenate([jnp.zeros((1, LANES), F32)] +
                          [jnp.sum(n_tile[0:e], axis=0, keepdims=True) for e in range(1, ne)], axis=0)
    pick = lambda sel, v: jnp.sum(jnp.where(sel, v, 0.0), axis=0, keepdims=True)
    r1 = pick(sel1, excl_tile + before[:, 0:1])
    r2 = pick(sel2, excl_tile + before[:, 0:1])
    p1 = pick(sel1, excl_tile + off[:, 0:1])
    p2 = pick(sel2, excl_tile + off[:, 0:1])
    carry_ref[...] = before + n_tile

    as_i = lambda v: v.astype(jnp.int32)
    oi_ref[...] = jnp.concatenate(
        [as_i(i1), as_i(i2), as_i(r1), as_i(r2), as_i(p1), as_i(p2), jnp.zeros((ne - 6, tl), jnp.int32)],
        axis=0)
    of_ref[...] = jnp.concatenate([gate1, gate2, jnp.zeros((ne - 2, tl), F32)], axis=0)
    lane = lax.broadcasted_iota(jnp.int32, (ne, LANES), 1)
    tile_ref[0] = as_i(jnp.where(lane == 0, n_tile, jnp.where(lane == 1, before, 0.0)))
    cnt_ref[...] = carry_ref[...]


def _route_call(logits_t):
    ne, t = logits_t.shape
    tl = min(TT_ROWS, t)
    blk = pl.BlockSpec((ne, tl), lambda i: (0, i))
    return pl.pallas_call(
        _route_kernel,
        grid=(t // tl,),
        in_specs=[blk],
        out_specs=[blk, blk, pl.BlockSpec((1, ne, LANES), lambda i: (i, 0, 0)),
                   pl.BlockSpec((ne, LANES), lambda i: (0, 0))],
        out_shape=[jax.ShapeDtypeStruct((ne, t), jnp.int32), jax.ShapeDtypeStruct((ne, t), F32),
                   jax.ShapeDtypeStruct((t // tl, ne, LANES), jnp.int32),
                   jax.ShapeDtypeStruct((ne, LANES), F32)],
        scratch_shapes=[pltpu.VMEM((ne, LANES), F32)],
        compiler_params=_cparams(("arbitrary",)),
        name="moe_route",
    )(logits_t)


def _run_pieces(n, max_rows):
    size = max_rows
    while size >= SUBLANES:
        yield size, pl.multiple_of(n & ~(2 * size - 1), SUBLANES), (n & size) != 0
        size //= 2


def _compact_rows(tt):
    return 2 * tt + N_EXPERTS * SUBLANES


def _dispatch_kernel(tail_ref, cnt_ref, base_ref, pos_ref, h_ref, xb_ref, c_ref, zero_ref, sem, zsem):
    step = pl.program_id(0)
    tt = h_ref.shape[0]
    tmb = zero_ref.shape[0]

    def zero_block(blk):
        row = pl.multiple_of(blk * tmb, tmb)
        return pltpu.make_async_copy(zero_ref, xb_ref.at[pl.ds(row, tmb)], zsem)

    @pl.when(step == 0)
    def _():
        zero_ref[...] = jnp.zeros_like(zero_ref)
        for e in range(N_EXPERTS):
            zero_block(tail_ref[e]).start()
        for e in range(N_EXPERTS):
            zero_block(tail_ref[e]).wait()

        def zero_unused(blk, c):
            cp = zero_block(blk)
            cp.start()
            cp.wait()
            return c

        lax.fori_loop(tail_ref[N_EXPERTS], xb_ref.shape[0] // tmb, zero_unused, 0)

    slot = step % 2
    r = lax.broadcasted_iota(jnp.int32, (c_ref.shape[1], tt), 0)
    onehot = jnp.where((r == pos_ref[4:5, :]) | (r == pos_ref[5:6, :]), 1.0, 0.0).astype(BF16)
    c_ref[slot] = jnp.dot(onehot, h_ref[...], preferred_element_type=F32)

    def run_copies(op, tile, buf):
        off = 0
        for e in range(N_EXPERTS):
            n = cnt_ref[tile * N_EXPERTS + e]
            dst = pl.multiple_of(base_ref[tile * N_EXPERTS + e], SUBLANES)
            for size, piece, present in _run_pieces(n, tt):
                @pl.when(present)
                def _():
                    src = pl.multiple_of(off + piece, SUBLANES)
                    op(pltpu.make_async_copy(c_ref.at[buf, pl.ds(src, size)],
                                             xb_ref.at[pl.ds(pl.multiple_of(dst + piece, SUBLANES), size)],
                                             sem.at[buf]))
            off = off + n

    run_copies(lambda cp: cp.start(), step, slot)

    @pl.when(step > 0)
    def _():
        run_copies(lambda cp: cp.wait(), step - 1, 1 - slot)

    @pl.when(step == pl.num_programs(0) - 1)
    def _():
        run_copies(lambda cp: cp.wait(), step, slot)


def _dispatch_call(tail_blocks, tile_cnt, tile_base, pos, h, n_slots):
    t, d = h.shape
    tt = min(TT_ROWS, t)
    tmb = TM_MOE
    return pl.pallas_call(
        _dispatch_kernel,
        grid_spec=pltpu.PrefetchScalarGridSpec(
            num_scalar_prefetch=3,
            grid=(t // tt,),
            in_specs=[pl.BlockSpec((N_EXPERTS, tt), lambda i, *_: (0, i)),
                      pl.BlockSpec((tt, d), lambda i, *_: (i, 0))],
            out_specs=pl.BlockSpec(memory_space=pl.ANY),
            scratch_shapes=[pltpu.VMEM((2, _compact_rows(tt), d), F32), pltpu.VMEM((tmb, d), F32),
                            pltpu.SemaphoreType.DMA((2,)), pltpu.SemaphoreType.DMA],
        ),
        out_shape=jax.ShapeDtypeStruct((n_slots, d), F32),
        compiler_params=_cparams(("arbitrary",)),
        name="moe_dispatch",
    )(tail_blocks, tile_cnt, tile_base, pos, h)


def _moe_ffn_kernel(be_ref, nb_ref, x_ref, wg_ref, wu_ref, wd_ref, o_ref, xb_ref):
    i = pl.program_id(0)
    c = pl.program_id(1)
    used = i < nb_ref[0]
    tf = wg_ref.shape[2]
    wpart = min(tf, MXU_DIM)
    parts = [((0, slice(None), slice(p, p + wpart)), (0, slice(p, p + wpart), slice(None)))
             for p in range(0, tf, wpart)]

    @pl.when(used & (c == 0))
    def _():
        xb_ref[...] = x_ref[...].astype(BF16)
        o_ref[...] = _swiglu_partial(xb_ref[...], wg_ref, wu_ref, wd_ref, parts)

    @pl.when(used & (c != 0))
    def _():
        o_ref[...] += _swiglu_partial(xb_ref[...], wg_ref, wu_ref, wd_ref, parts)

    @pl.when(jnp.logical_not(used) & (c == 0))
    def _():
        o_ref[...] = jnp.zeros_like(o_ref)


def _moe_ffn_call(block_e, n_used, xb, wg, wu, wd):
    n_slots, d = xb.shape
    tmb = TM_MOE
    nb = n_slots // tmb
    f = wg.shape[2]
    tf = _largest_divisor(f, (TF_MOE, 512, 256, 128))
    nc = f // tf
    blk = lambda i, nbu: jnp.minimum(i, nbu[0] - 1)
    chunk = lambda i, c, nbu: jnp.where(i < nbu[0], c, nc - 1)
    return pl.pallas_call(
        _moe_ffn_kernel,
        grid_spec=pltpu.PrefetchScalarGridSpec(
            num_scalar_prefetch=2,
            grid=(nb, nc),
            in_specs=[
                pl.BlockSpec((tmb, d), lambda i, c, be, nbu: (blk(i, nbu), 0)),
                pl.BlockSpec((1, d, tf), lambda i, c, be, nbu: (be[blk(i, nbu)], 0, chunk(i, c, nbu))),
                pl.BlockSpec((1, d, tf), lambda i, c, be, nbu: (be[blk(i, nbu)], 0, chunk(i, c, nbu))),
                pl.BlockSpec((1, tf, d), lambda i, c, be, nbu: (be[blk(i, nbu)], chunk(i, c, nbu), 0)),
            ],
            out_specs=pl.BlockSpec((tmb, d), lambda i, c, be, nbu: (i, 0)),
            scratch_shapes=[pltpu.VMEM((tmb, d), BF16)],
        ),
        out_shape=jax.ShapeDtypeStruct((n_slots, d), F32),
        compiler_params=_cparams(("arbitrary", "arbitrary")),
        name="moe_ffn",
    )(block_e, n_used, xb, wg, wu, wd)


def _combine_kernel(dest_ref, yb_ref, x_ref, g2_ref, gate_ref, o_ref, rows_ref, sem):
    tt = x_ref.shape[1]

    def row_copy(t, k):
        return pltpu.make_async_copy(yb_ref.at[pl.ds(dest_ref[0, k, t], 1)],
                                     rows_ref.at[k, pl.ds(t, 1)], sem)

    def wait(t, c):
        row_copy(t, 0).wait()
        row_copy(t, 1).wait()
        return c

    for t in range(tt):
        row_copy(t, 0).start(priority=0)
        row_copy(t, 1).start(priority=1)
    lax.fori_loop(0, tt, wait, 0, unroll=8)
    y = rows_ref[0] * gate_ref[0, :, 0:1] + rows_ref[1] * gate_ref[0, :, 1:2]
    o_ref[0] = x_ref[0] + g2_ref[0] * y


def _combine_call(dest, yb, x, g2, gates):
    b, s, d = x.shape
    tt = dest.shape[2]
    nj = s // tt
    row_spec = pl.BlockSpec((1, tt, d), lambda i, j: (i, j, 0))
    return pl.pallas_call(
        _combine_kernel,
        grid=(b, nj),
        in_specs=[
            pl.BlockSpec((1, 2, tt), lambda i, j: (i * nj + j, 0, 0), memory_space=pltpu.SMEM),
            pl.BlockSpec(memory_space=pl.ANY),
            row_spec,
            pl.BlockSpec((1, 1, d), lambda i, j: (i, 0, 0)),
            pl.BlockSpec((1, tt, 2), lambda i, j: (i * nj + j, 0, 0)),
        ],
        out_specs=row_spec,
        out_shape=jax.ShapeDtypeStruct((b, s, d), F32),
        scratch_shapes=[pltpu.VMEM((2, tt, d), F32), pltpu.SemaphoreType.DMA],
        compiler_params=_cparams(("arbitrary", "arbitrary")),
        name="moe_combine",
    )(dest, yb, x, g2, gates)


def _chunk_cols(w, tf):
    d, f = w.shape
    return w.reshape(d, f // tf, tf).transpose(1, 0, 2).astype(BF16)


def _moe_layer(h, x1, g2, logits_t, wg, wu, wd):
    b, s, d = x1.shape
    t = b * s
    tmb = TM_MOE
    tt = min(TT_ROWS, s)
    ne = N_EXPERTS
    oi, of, tiles, cnt = _route_call(logits_t)
    e1, e2, r1, r2 = oi[0], oi[1], oi[2], oi[3]
    counts = cnt[:, 0].astype(jnp.int32)
    padded = (counts + tmb - 1) // tmb * tmb
    pad_end = jnp.cumsum(padded)
    pad_start = pad_end - padded
    n_used = (pad_end[-1] // tmb).astype(jnp.int32)
    nb = -(-(2 * t + (t // tt) * ne * (SUBLANES - 1)) // tmb) + ne
    block_row0 = jnp.arange(nb, dtype=jnp.int32) * tmb
    block_e = jnp.minimum(jnp.sum(pad_end[None, :] <= block_row0[:, None], axis=1), ne - 1).astype(jnp.int32)
    tail_blocks = jnp.where(padded > 0, pad_end // tmb - 1, nb - 1)
    tail_blocks = jnp.concatenate([tail_blocks, n_used[None]]).astype(jnp.int32)
    start_of = lambda e: jnp.sum(jnp.where(e[None, :] == jnp.arange(ne)[:, None], pad_start[:, None], 0), axis=0)
    d1 = start_of(e1) + r1
    d2 = start_of(e2) + r2
    dest = jnp.stack([d1.reshape(t // tt, tt), d2.reshape(t // tt, tt)], axis=1).astype(jnp.int32)

    tile_cnt = tiles[:, :, 0].reshape(-1)
    tile_base = (tiles[:, :, 1] + pad_start[None, :]).reshape(-1).astype(jnp.int32)
    xb = _dispatch_call(tail_blocks, tile_cnt, tile_base, oi, h.reshape(t, d), nb * tmb)
    yb = _moe_ffn_call(block_e, n_used.reshape(1), xb, wg, wu, wd)
    gates = jnp.stack([of[0], of[1]], axis=-1).reshape(t // tt, tt, 2)
    return _combine_call(dest, yb, x1, g2, gates)


def kernel(x, c, ada_w, ada_b, norm_mix_g, norm_ffn_g, w_in, q_norm_g, k_norm_g, sg_ln_g, sg_ln_b,
           sg_w_spatial, sg_b_spatial, w_out, ffn_w_gate, ffn_w_up, ffn_w_down, router_w, router_b,
           moe_w_gate, moe_w_up, moe_w_down):
    b, s, d = x.shape
    depth = ada_w.shape[0]
    sgw = sg_ln_g.shape[-1]
    sbw = w_out.shape[1] - sgw
    assert sbw % PAIR == 0 and sgw % PAIR == 0 and s % SEQ_BLOCK == 0
    assert w_in.shape[-1] == 3 * sbw + 2 * sgw

    ada = _ada_call(c, ada_w, ada_b)
    ada = ada.reshape(depth, b, 6, d)

    wb = MXU_DIM if sbw % MXU_DIM == 0 else PAIR
    gi = jnp.arange(wb) // HEAD_DIM
    bd = (gi[:, None] == gi[None, :]).astype(BF16)
    pos = jnp.arange(SEQ_BLOCK)
    chunk_causal = (pos[None, :] // CHUNK) <= (pos[:, None] // CHUNK)
    scale = HEAD_DIM ** -0.5

    for l in range(depth):
        sh1, sc1, g1, sh2, sc2, g2 = (ada[l, :, j] for j in range(6))
        qk_g = jnp.stack([jnp.tile(q_norm_g[l], sbw // HEAD_DIM) * scale,
                          jnp.tile(k_norm_g[l], sbw // HEAD_DIM)])
        ln = jnp.stack([sg_ln_g[l], sg_ln_b[l]])
        ws = jnp.where(chunk_causal[None], sg_w_spatial[l], 0.0)
        ws = ws.reshape(sgw // PAIR, 2, SEQ_BLOCK, SEQ_BLOCK).transpose(0, 2, 1, 3)
        ws = ws.reshape(sgw // PAIR, SEQ_BLOCK, 2 * SEQ_BLOCK).astype(BF16)
        bias = jnp.repeat(sg_b_spatial[l].T, HEAD_DIM, axis=1)

        q, k, v, ob = _inproj_call(x, jnp.stack([sh1, sc1], axis=1), norm_mix_g[l][None],
                                   w_in[l].astype(BF16), qk_g, bd, ln, ws, bias, sbw=sbw, sgw=sgw)
        oa = _attn_call(q, k, v)
        mod2 = jnp.stack([g1, sh2, sc2], axis=1)
        j = l // 2
        if l % 2 == 0:
            x1, h = _outproj_call(oa, ob, x, mod2, norm_ffn_g[l][None], w_out[l].astype(BF16))
            tf = _largest_divisor(ffn_w_gate.shape[-1], (512, 256, 128))
            wg = _chunk_cols(ffn_w_gate[j], tf)
            wu = _chunk_cols(ffn_w_up[j], tf)
            wd = ffn_w_down[j].reshape(-1, tf, d).astype(BF16)
            x = _ffn_call(h, x1, g2[:, None], wg, wu, wd)
        else:
            rw_t = jnp.zeros((2 * N_EXPERTS, d), BF16).at[:N_EXPERTS].set(router_w[j].T.astype(BF16))
            rb = router_b[j].reshape(N_EXPERTS, 1)
            x1, h, logits_t = _outproj_call(oa, ob, x, mod2, norm_ffn_g[l][None],
                                            w_out[l].astype(BF16), router=(rw_t, rb))
            x = _moe_layer(h, x1, g2[:, None], logits_t, moe_w_gate[j].astype(BF16),
                           moe_w_up[j].astype(BF16), moe_w_down[j].astype(BF16))
    return x
```
